```python
import math
import jax, jax.numpy as jnp
from jax import lax
import numpy as np

D_MODEL = 1024
BATCH = 8
SEQ = 4096
DEPTH = 4

N_HEADS_A = D_MODEL // 128
HEAD_DIM_A = 64
A_QK_WIDTH = N_HEADS_A * 2 * HEAD_DIM_A
A_V_WIDTH = N_HEADS_A * 2 * HEAD_DIM_A
N_HEADS_B = D_MODEL // 128
DK_B = 128
DV_B = 128
B_QK_WIDTH = N_HEADS_B * DK_B
B_V_WIDTH = N_HEADS_B * DV_B
CONV_B = 4
CHUNK_B = 64
N_BRANCH = 2
D_FF = ((8 * D_MODEL // 3 + 127) // 128) * 128
CONV_FF = 3
D_PLE = 256
N_BUCKETS = 32
MAX_DISTANCE = 128
Q_BLOCK = 128
EPS = 1e-6
NEG_INF = -1e30

SPLITS = (A_QK_WIDTH, A_QK_WIDTH, A_V_WIDTH, B_QK_WIDTH, B_QK_WIDTH, B_V_WIDTH, B_V_WIDTH, N_HEADS_B, N_HEADS_B, N_BRANCH * D_MODEL)
D_IN = sum(SPLITS)

kernel_name = "hybrid_diffattn_gdn_convffn_block"


def rmsnorm(x, g):
    xf = x.astype(jnp.float32)
    y = xf * lax.rsqrt(jnp.mean(xf * xf, axis=-1, keepdims=True) + EPS)
    return (y * g.astype(jnp.float32)).astype(x.dtype)


def l2norm(x):
    xf = x.astype(jnp.float32)
    return xf * lax.rsqrt(jnp.sum(xf * xf, axis=-1, keepdims=True) + EPS)


def causal_dwconv(x, w):
    K = w.shape[0]
    S = x.shape[1]
    xp = jnp.pad(x, ((0, 0), (K - 1, 0), (0, 0)))
    y = xp[:, 0:S] * w[0]
    for j in range(1, K):
        y = y + xp[:, j:j + S] * w[j]
    return y


def rel_bucket(rel):
    n = jnp.maximum(rel, 0)
    max_exact = N_BUCKETS // 2
    nf = jnp.maximum(n, 1).astype(jnp.float32)
    large = max_exact + (jnp.log(nf / max_exact) / math.log(MAX_DISTANCE / max_exact) * (N_BUCKETS - max_exact)).astype(jnp.int32)
    large = jnp.minimum(large, N_BUCKETS - 1)
    return jnp.where(n < max_exact, n, large)


def diff_attention(q, k, v, lam, bias_table):
    B, S, H, _, d = q.shape
    nb = S // Q_BLOCK
    qb = q.reshape(B, nb, Q_BLOCK, H, 2, d).transpose(1, 0, 2, 3, 4, 5)
    k_pos = jnp.arange(S)

    def block(args):
        idx, q_blk = args
        q_pos = idx * Q_BLOCK + jnp.arange(Q_BLOCK)
        rel = q_pos[:, None] - k_pos[None, :]
        bias = bias_table[rel_bucket(rel)].astype(jnp.float32).transpose(2, 0, 1)
        logits = jnp.einsum('bqhjd,bkhjd->bhjqk', q_blk, k, preferred_element_type=jnp.float32)
        logits = logits + bias[None, :, None]
        logits = jnp.where((rel >= 0)[None, None, None], logits, NEG_INF)
        probs = jax.nn.softmax(logits, axis=-1)
        w = probs[:, :, 0] - lam * probs[:, :, 1]
        return jnp.einsum('bhqk,bkhe->bqhe', w.astype(v.dtype), v)

    out = lax.map(block, (jnp.arange(nb), qb))
    return out.transpose(1, 0, 2, 3, 4).reshape(B, S, H, 2 * d)


def gated_delta_rule(q, k, v, g, beta):
    B, S, H, dk = q.shape
    dv = v.shape[-1]
    C = CHUNK_B
    N = S // C

    def to_chunks(t):
        return t.reshape(B, N, C, H, t.shape[-1]).transpose(1, 0, 3, 2, 4)

    q, k, v = to_chunks(q), to_chunks(k), to_chunks(v)
    g = jnp.cumsum(g.reshape(B, N, C, H).transpose(1, 0, 3, 2), axis=-1)
    beta = beta.reshape(B, N, C, H).transpose(1, 0, 3, 2)
    tril = jnp.tril(jnp.ones((C, C), dtype=bool))
    strict = jnp.tril(jnp.ones((C, C), dtype=bool), -1)
    decay = jnp.exp(jnp.where(tril, g[..., :, None] - g[..., None, :], -jnp.inf))
    k_beta = k * beta[..., None]
    v_beta = v * beta[..., None]
    kkT = jnp.einsum('nbhid,nbhjd->nbhij', k_beta, k)
    A = jnp.eye(C, dtype=jnp.float32) + jnp.where(strict, kkT * decay, 0.0)
    rhs = jnp.concatenate([v_beta, k_beta * jnp.exp(g)[..., None]], axis=-1)
    sol = lax.linalg.triangular_solve(A, rhs, left_side=True, lower=True, unit_diagonal=True)
    u = sol[..., :dv]
    w = sol[..., dv:]
    attn_intra = jnp.where(tril, jnp.einsum('nbhid,nbhjd->nbhij', q, k) * decay, 0.0)

    def step(state, inp):
        q_c, k_c, u_c, w_c, g_c, a_c = inp
        v_new = u_c - w_c @ state
        o = (q_c * jnp.exp(g_c)[..., None]) @ state + a_c @ v_new
        g_last = g_c[..., -1]
        k_dec = k_c * jnp.exp(g_last[..., None] - g_c)[..., None]
        state = state * jnp.exp(g_last)[..., None, None] + jnp.swapaxes(k_dec, -1, -2) @ v_new
        return state, o

    s0 = jnp.zeros((B, H, dk, dv), dtype=jnp.float32)
    _, o = lax.scan(step, s0, (q, k, u, w, g, attn_intra))
    return o.transpose(1, 0, 3, 2, 4).reshape(B, S, H, dv)


def setup_inputs(seed: int = 0) -> dict:
    key = jax.random.key(seed)
    ks = jax.random.split(key, 26)
    f32 = jnp.float32
    nrm = lambda k, shape, s: jax.random.normal(k, shape, f32) * s
    gain = lambda k, shape: 1.0 + 0.02 * jax.random.normal(k, shape, f32)
    dt = jnp.exp(jax.random.uniform(ks[10], (DEPTH, N_HEADS_B), f32, math.log(1e-3), math.log(1e-1)))
    return {
        "x": nrm(ks[0], (BATCH, SEQ, D_MODEL), 1.0),
        "p": nrm(ks[1], (DEPTH, BATCH, SEQ, D_PLE), 1.0),
        "w_in": nrm(ks[2], (DEPTH, D_MODEL, D_IN), D_MODEL ** -0.5),
        "attn_norm": gain(ks[3], (DEPTH, D_MODEL)),
        "qnorm_a": gain(ks[4], (DEPTH, HEAD_DIM_A)),
        "knorm_a": gain(ks[5], (DEPTH, HEAD_DIM_A)),
        "lam_a": nrm(ks[6], (DEPTH, 4, HEAD_DIM_A), 0.1),
        "subln_a": gain(ks[7], (DEPTH, 2 * HEAD_DIM_A)),
        "rel_bias": nrm(ks[8], (N_BUCKETS, N_HEADS_A), 0.3),
        "conv_b": nrm(ks[9], (DEPTH, CONV_B, 2 * B_QK_WIDTH + B_V_WIDTH), CONV_B ** -0.5),
        "a_log_b": jnp.log(jax.random.uniform(ks[11], (DEPTH, N_HEADS_B), f32, 1.0, 16.0)),
        "dt_bias_b": dt + jnp.log(-jnp.expm1(-dt)),
        "onorm_b": gain(ks[12], (DEPTH, DV_B)),
        "w_br_a": nrm(ks[13], (DEPTH, A_V_WIDTH, D_MODEL), A_V_WIDTH ** -0.5),
        "w_br_b": nrm(ks[14], (DEPTH, B_V_WIDTH, D_MODEL), B_V_WIDTH ** -0.5),
        "w_out": nrm(ks[15], (DEPTH, D_MODEL, D_MODEL), D_MODEL ** -0.5),
        "ffn_norm": gain(ks[16], (DEPTH, D_MODEL)),
        "w_up": nrm(ks[17], (DEPTH, D_MODEL, 2 * D_FF), D_MODEL ** -0.5),
        "conv_ff": nrm(ks[18], (DEPTH, CONV_FF, 2 * D_FF), CONV_FF ** -0.5),
        "conv_ff_bias": nrm(ks[19], (DEPTH, 2 * D_FF), 0.01),
        "w_down": nrm(ks[20], (DEPTH, D_FF, D_MODEL), D_FF ** -0.5),
        "ple_norm": gain(ks[21], (DEPTH, D_MODEL)),
        "w_ple_gate": nrm(ks[22], (DEPTH, D_MODEL, D_MODEL), D_MODEL ** -0.5),
        "w_ple_proj": nrm(ks[23], (DEPTH, D_PLE, D_MODEL), D_PLE ** -0.5),
    }


def reference(x, p, w_in, attn_norm, qnorm_a, knorm_a, lam_a, subln_a, rel_bias, conv_b, a_log_b, dt_bias_b, onorm_b, w_br_a, w_br_b, w_out, ffn_norm, w_up, conv_ff, conv_ff_bias, w_down, ple_norm, w_ple_gate, w_ple_proj):
    Bsz, S, _ = x.shape
    cuts = np.cumsum(SPLITS)[:-1].tolist()
    for i in range(DEPTH):
        lam_init = 0.8 - 0.6 * math.exp(-0.3 * i)
        h = rmsnorm(x, attn_norm[i])
        proj = h @ w_in[i]
        qa, ka, va, qb, kb, vb, zb, bb, ab, gl = jnp.split(proj, cuts, axis=-1)

        qa = rmsnorm(qa.reshape(Bsz, S, N_HEADS_A, 2, HEAD_DIM_A), qnorm_a[i]) * (HEAD_DIM_A ** -0.5)
        ka = rmsnorm(ka.reshape(Bsz, S, N_HEADS_A, 2, HEAD_DIM_A), knorm_a[i])
        lq = lam_a[i].astype(jnp.float32)
        lam = jnp.exp(jnp.sum(lq[0] * lq[1])) - jnp.exp(jnp.sum(lq[2] * lq[3])) + lam_init
        oa = diff_attention(qa, ka, va.reshape(Bsz, S, N_HEADS_A, 2 * HEAD_DIM_A), lam, rel_bias)
        oa = (rmsnorm(oa, subln_a[i]) * (1.0 - lam_init)).reshape(Bsz, S, A_V_WIDTH)

        qkv = jax.nn.silu(causal_dwconv(jnp.concatenate([qb, kb, vb], axis=-1), conv_b[i]))
        qb, kb, vb = jnp.split(qkv, [B_QK_WIDTH, 2 * B_QK_WIDTH], axis=-1)
        qb = l2norm(qb.reshape(Bsz, S, N_HEADS_B, DK_B)) * (DK_B ** -0.5)
        kb = l2norm(kb.reshape(Bsz, S, N_HEADS_B, DK_B))
        vb = vb.reshape(Bsz, S, N_HEADS_B, DV_B).astype(jnp.float32)
        beta = jax.nn.sigmoid(bb.astype(jnp.float32))
        g = -jnp.exp(a_log_b[i].astype(jnp.float32)) * jax.nn.softplus(ab.astype(jnp.float32) + dt_bias_b[i].astype(jnp.float32))
        ob = gated_delta_rule(qb, kb, vb, g, beta).astype(x.dtype)
        ob = (rmsnorm(ob, onorm_b[i]) * jax.nn.silu(zb.reshape(Bsz, S, N_HEADS_B, DV_B))).reshape(Bsz, S, B_V_WIDTH)

        gates = jax.nn.sigmoid(gl.reshape(Bsz, S, N_BRANCH, D_MODEL))
        mixed = gates[:, :, 0] * (oa @ w_br_a[i]) + gates[:, :, 1] * (ob @ w_br_b[i])
        x = x + mixed @ w_out[i]

        h = rmsnorm(x, ffn_norm[i])
        u = causal_dwconv(h @ w_up[i], conv_ff[i]) + conv_ff_bias[i]
        ug, uv = jnp.split(u, 2, axis=-1)
        x = x + (jax.nn.silu(ug) * uv) @ w_down[i]

        h = rmsnorm(x, ple_norm[i])
        x = x + jax.nn.sigmoid(h @ w_ple_gate[i]) * (p[i] @ w_ple_proj[i])
    return x
```

```python
import functools
import math

import numpy as np
import jax
import jax.numpy as jnp
from jax import lax
from jax.experimental import pallas as pl
from jax.experimental.pallas import tpu as pltpu

F32 = jnp.float32
BF16 = jnp.bfloat16

D_MODEL = 1024
N_HEADS = 8
HEAD_W = 128
HEAD_DIM_A = 64
DK_B = 128
CONV_B = 4
CHUNK_B = 64
D_FF = 2816
CONV_FF = 3
D_PLE = 256
N_BUCKETS = 32
MAX_DISTANCE = 128
EPS = 1e-6
NEG_INF = -1e30

LANE = 128
SUBLANE = 8
VMEM_LIMIT_BYTES = 56 * 1024 * 1024

PROJ_ROWS = 1024
PROJ_COLS = 1024
ATT_BLOCK = 512
GDN_ROWS = 256
PAIR = 2 * CHUNK_B
MERGE_ROWS = 512
FFN_ROWS = 512
FFN_COLS = 256
HALO = SUBLANE


def _bucket_thresholds():
    n = np.arange(0, 4 * MAX_DISTANCE)
    max_exact = N_BUCKETS // 2
    nf = np.maximum(n, 1).astype(np.float32)
    large = max_exact + (np.log(nf / np.float32(max_exact)) / np.float32(math.log(MAX_DISTANCE / max_exact))
                         * np.float32(N_BUCKETS - max_exact)).astype(np.int32)
    bucket = np.where(n < max_exact, n, np.minimum(large, N_BUCKETS - 1))
    return [int(np.argmax(bucket >= b)) for b in range(1, N_BUCKETS)]


BUCKET_LO = _bucket_thresholds()


def _params(semantics):
    return pltpu.CompilerParams(dimension_semantics=semantics, vmem_limit_bytes=VMEM_LIMIT_BYTES)


def _resident(shape):
    zeros = (0,) * len(shape)
    return pl.BlockSpec(shape, lambda *_: zeros, pipeline_mode=pl.Buffered(1))


def _rmsnorm(x, g):
    ms = jnp.mean(x * x, axis=-1, keepdims=True)
    return x * lax.rsqrt(ms + EPS) * g


def _sigmoid(x):
    return 1.0 / (1.0 + jnp.exp(-x))


def _silu(x):
    return x * _sigmoid(x)


def _dot(a, b):
    return jnp.dot(a, b, preferred_element_type=F32)


def _dot_nt(a, b):
    return lax.dot_general(a, b, (((1,), (1,)), ((), ())), preferred_element_type=F32)


def _attn_proj_kernel(x_ref, g_ref, w_ref, gain_ref, gsum_ref, gexp_ref, o_ref, h_ref):
    j = pl.program_id(1)

    @pl.when(j == 0)
    def _():
        h_ref[...] = _rmsnorm(x_ref[...], g_ref[...]).astype(BF16)

    y = _dot(h_ref[...], w_ref[...])

    @pl.when(j < 2)
    def _():
        ss = _dot((y * y).astype(BF16), gsum_ref[...])
        r = lax.rsqrt(ss * (1.0 / HEAD_DIM_A) + EPS)
        r_hi = r.astype(BF16)
        r_lo = (r - r_hi.astype(F32)).astype(BF16)
        rb = _dot(jnp.concatenate([r_hi, r_lo], axis=1), gexp_ref[...])
        o_ref[...] = (y * rb * gain_ref[...]).astype(BF16)

    @pl.when(j == 2)
    def _():
        o_ref[...] = y.astype(BF16)


def _attn_proj(x2, g, w, gains):
    rows = x2.shape[0]
    n_groups = D_MODEL // HEAD_DIM_A
    col = np.arange(D_MODEL) // HEAD_DIM_A
    gsum = (col[:, None] == np.arange(LANE)[None, :]).astype(np.float32)
    gexp = np.concatenate([gsum.T, gsum.T], axis=0)
    assert n_groups <= LANE
    return pl.pallas_call(
        _attn_proj_kernel,
        grid=(rows // PROJ_ROWS, 3),
        in_specs=[
            pl.BlockSpec((PROJ_ROWS, D_MODEL), lambda i, j: (i, 0)),
            pl.BlockSpec((1, D_MODEL), lambda i, j: (0, 0)),
            pl.BlockSpec((D_MODEL, PROJ_COLS), lambda i, j: (0, j)),
            pl.BlockSpec((None, 1, PROJ_COLS), lambda i, j: (j, 0, 0)),
            pl.BlockSpec((D_MODEL, LANE), lambda i, j: (0, 0)),
            pl.BlockSpec((2 * LANE, D_MODEL), lambda i, j: (0, 0)),
        ],
        out_specs=pl.BlockSpec((PROJ_ROWS, PROJ_COLS), lambda i, j: (i, j)),
        out_shape=jax.ShapeDtypeStruct((rows, 3 * D_MODEL), BF16),
        scratch_shapes=[pltpu.VMEM((PROJ_ROWS, D_MODEL), BF16)],
        compiler_params=_params(("parallel", "arbitrary")),
        name="attn_proj",
    )(x2, g, w, gains, jnp.asarray(gsum, BF16), jnp.asarray(gexp, BF16))


def _gdn_proj_kernel(x_ref, g_ref, w_ref, wsm_ref, o_ref, osm_ref, h_ref):
    j = pl.program_id(1)

    @pl.when(j == 0)
    def _():
        h = _rmsnorm(x_ref[...], g_ref[...]).astype(BF16)
        h_ref[...] = h
        osm_ref[...] = _dot(h, wsm_ref[...])

    o_ref[...] = _dot(h_ref[...], w_ref[...])


def _gdn_proj(x2, g, w, wsm):
    rows = x2.shape[0]
    n_cols = w.shape[1]
    return pl.pallas_call(
        _gdn_proj_kernel,
        grid=(rows // PROJ_ROWS, n_cols // PROJ_COLS),
        in_specs=[
            pl.BlockSpec((PROJ_ROWS, D_MODEL), lambda i, j: (i, 0)),
            pl.BlockSpec((1, D_MODEL), lambda i, j: (0, 0)),
            pl.BlockSpec((D_MODEL, PROJ_COLS), lambda i, j: (0, j)),
            pl.BlockSpec((D_MODEL, LANE), lambda i, j: (0, 0)),
        ],
        out_specs=[
            pl.BlockSpec((PROJ_ROWS, PROJ_COLS), lambda i, j: (i, j)),
            pl.BlockSpec((PROJ_ROWS, LANE), lambda i, j: (i, 0)),
        ],
        out_shape=[
            jax.ShapeDtypeStruct((rows, n_cols), F32),
            jax.ShapeDtypeStruct((rows, LANE), F32),
        ],
        scratch_shapes=[pltpu.VMEM((PROJ_ROWS, D_MODEL), BF16)],
        compiler_params=_params(("parallel", "arbitrary")),
        name="gdn_proj",
    )(x2, g, w, wsm)


def _bias_tiles_kernel(tab_ref, o_ref):
    h = pl.program_id(0)
    t = o_ref.shape[2]
    row = lax.broadcasted_iota(jnp.int32, (t, t), 0)
    col = lax.broadcasted_iota(jnp.int32, (t, t), 1)
    for idx in range(2):
        rel = row - col + idx * t
        b = jnp.full((t, t), tab_ref[h, 0], F32)
        for bk in range(1, N_BUCKETS):
            b = jnp.where(rel >= BUCKET_LO[bk - 1], tab_ref[h, bk], b)
        if idx == 0:
            b = jnp.where(rel >= 0, b, NEG_INF)
        o_ref[0, idx] = b


def _bias_tiles(rel_bias_t):
    t = ATT_BLOCK
    return pl.pallas_call(
        _bias_tiles_kernel,
        grid=(N_HEADS,),
        in_specs=[pl.BlockSpec(memory_space=pltpu.SMEM)],
        out_specs=pl.BlockSpec((1, 2, t, t), lambda h: (h, 0, 0, 0)),
        out_shape=jax.ShapeDtypeStruct((N_HEADS, 2, t, t), F32),
        compiler_params=_params(("parallel",)),
        name="bias_tiles",
    )(rel_bias_t)


def _attn_kernel(qi_tab, kj_tab, q_ref, k_ref, v_ref, bias_ref, tab_ref, lam_ref, sg_ref, li_ref,
                 o_ref, qs_ref, m_ref, l_ref, acc_ref):
    h = pl.program_id(0)
    step = pl.program_id(2)
    qi = qi_tab[step]
    kj = kj_tab[step]
    t = q_ref.shape[1]

    @pl.when(kj == 0)
    def _():
        q = q_ref[0]
        lane = lax.broadcasted_iota(jnp.int32, q.shape, 1)
        zero = jnp.zeros_like(q)
        qs_ref[0:t, :] = jnp.where(lane < HEAD_DIM_A, q, zero)
        qs_ref[t:2 * t, :] = jnp.where(lane >= HEAD_DIM_A, q, zero)
        m_ref[...] = jnp.full(m_ref.shape, NEG_INF, F32)
        l_ref[...] = jnp.zeros(l_ref.shape, F32)
        acc_ref[...] = jnp.zeros(acc_ref.shape, F32)

    s = _dot_nt(qs_ref[...], k_ref[0])

    def update(s2):
        m_prev = m_ref[...]
        m_new = jnp.maximum(m_prev, jnp.max(s2, axis=-1, keepdims=True))
        alpha = jnp.exp(m_prev - m_new)
        p = jnp.exp(s2 - m_new)
        l_ref[...] = alpha * l_ref[...] + jnp.sum(p, axis=-1, keepdims=True)
        acc_ref[...] = alpha * acc_ref[...] + _dot(p.astype(BF16), v_ref[0])
        m_ref[...] = m_new

    delta = qi - kj

    @pl.when(delta >= 2)
    def _():
        update(s + tab_ref[h, N_BUCKETS - 1])

    @pl.when(delta == 1)
    def _():
        b = bias_ref[0, 1]
        update(s + jnp.concatenate([b, b], axis=0))

    @pl.when(delta == 0)
    def _():
        b = bias_ref[0, 0]
        update(s + jnp.concatenate([b, b], axis=0))
        acc = acc_ref[...]
        l = l_ref[...]
        o0 = acc[0:t] / l[0:t]
        o1 = acc[t:2 * t] / l[t:2 * t]
        lq = lam_ref[...]
        lam_init = li_ref[0]
        lam = (jnp.exp(jnp.sum(lq[0:1] * lq[1:2], axis=-1, keepdims=True))
               - jnp.exp(jnp.sum(lq[2:3] * lq[3:4], axis=-1, keepdims=True)) + lam_init)
        o = o0 - lam * o1
        o_ref[0] = (_rmsnorm(o, sg_ref[...]) * (1.0 - lam_init)).astype(BF16)


def _attention(qkv, bias_tiles, rel_bias_t, lam_a, subln, lam_init):
    bsz, seq, _ = qkv.shape
    t = ATT_BLOCK
    nq = seq // t
    assert t + 1 >= BUCKET_LO[-1], "tiles two or more blocks below the diagonal must sit in the last bucket"
    pairs = [(qi, kj) for qi in range(nq) for kj in range(qi + 1)]
    qi_tab = jnp.asarray([p[0] for p in pairs], jnp.int32)
    kj_tab = jnp.asarray([p[1] for p in pairs], jnp.int32)
    grid_spec = pltpu.PrefetchScalarGridSpec(
        num_scalar_prefetch=2,
        grid=(N_HEADS, bsz, len(pairs)),
        in_specs=[
            pl.BlockSpec((1, t, HEAD_W), lambda h, b, s, qt, kt: (b, qt[s], h)),
            pl.BlockSpec((1, t, HEAD_W), lambda h, b, s, qt, kt: (b, kt[s], N_HEADS + h)),
            pl.BlockSpec((1, t, HEAD_W), lambda h, b, s, qt, kt: (b, kt[s], 2 * N_HEADS + h)),
            pl.BlockSpec((1, 2, t, t), lambda h, b, s, qt, kt: (h, 0, 0, 0)),
            pl.BlockSpec(memory_space=pltpu.SMEM),
            pl.BlockSpec((4, HEAD_DIM_A), lambda h, b, s, qt, kt: (0, 0)),
            pl.BlockSpec((1, HEAD_W), lambda h, b, s, qt, kt: (0, 0)),
            pl.BlockSpec(memory_space=pltpu.SMEM),
        ],
        out_specs=pl.BlockSpec((1, t, HEAD_W), lambda h, b, s, qt, kt: (b, qt[s], h)),
        scratch_shapes=[
            pltpu.VMEM((2 * t, HEAD_W), BF16),
            pltpu.VMEM((2 * t, 1), F32),
            pltpu.VMEM((2 * t, 1), F32),
            pltpu.VMEM((2 * t, HEAD_W), F32),
        ],
    )
    return pl.pallas_call(
        _attn_kernel,
        grid_spec=grid_spec,
        out_shape=jax.ShapeDtypeStruct((bsz, seq, D_MODEL), BF16),
        compiler_params=_params(("parallel", "parallel", "arbitrary")),
        name="diff_attention",
    )(qi_tab, kj_tab, qkv, qkv, qkv, bias_tiles, rel_bias_t, lam_a, subln, lam_init)


def _gdn_kernel(alog_ref, dtb_ref, q_ref, k_ref, v_ref, z_ref, sm_ref, cwq_ref, cwk_ref, cwv_ref, on_ref,
                o_ref, state_ref, halo_ref):
    h = pl.program_id(1)
    rows = q_ref.shape[1]
    n_pairs = rows // PAIR

    @pl.when(pl.program_id(2) == 0)
    def _():
        state_ref[...] = jnp.zeros(state_ref.shape, F32)
        halo_ref[...] = jnp.zeros(halo_ref.shape, F32)

    def conv_silu(x_ref, w_ref, idx):
        x = x_ref[0]
        xe = jnp.concatenate([halo_ref[idx], x], axis=0)
        w = w_ref[...]
        off = HALO - (CONV_B - 1)
        y = w[0:1] * xe[off:off + rows]
        for tap in range(1, CONV_B):
            y = y + w[tap:tap + 1] * xe[off + tap:off + tap + rows]
        halo_ref[idx] = x[rows - HALO:rows]
        return _silu(y)

    q = conv_silu(q_ref, cwq_ref, 0)
    k = conv_silu(k_ref, cwk_ref, 1)
    v = conv_silu(v_ref, cwv_ref, 2)
    q = q * lax.rsqrt(jnp.sum(q * q, axis=-1, keepdims=True) + EPS) * (DK_B ** -0.5)
    k = k * lax.rsqrt(jnp.sum(k * k, axis=-1, keepdims=True) + EPS)

    sm = sm_ref[0]
    lane = lax.broadcasted_iota(jnp.int32, sm.shape, 1)
    b_raw = jnp.sum(jnp.where(lane == h, sm, 0.0), axis=-1, keepdims=True)
    a_raw = jnp.sum(jnp.where(lane == h + N_HEADS, sm, 0.0), axis=-1, keepdims=True)
    beta = _sigmoid(b_raw)
    xa = a_raw + dtb_ref[h]
    softplus = jnp.maximum(xa, 0.0) + jnp.log1p(jnp.exp(-jnp.abs(xa)))
    decay_rate = jnp.exp(jnp.full((1, LANE), alog_ref[h], F32))
    g = jnp.broadcast_to(-softplus, (rows, LANE)) * decay_rate

    pos = lax.broadcasted_iota(jnp.int32, (rows, LANE), 0) & (CHUNK_B - 1)
    shift = 1
    while shift < CHUNK_B:
        g = g + jnp.where(pos >= shift, pltpu.roll(g, shift, axis=0), 0.0)
        shift *= 2
    gc = g.reshape(n_pairs, PAIR, LANE)
    g_last = jnp.concatenate(
        [jnp.broadcast_to(gc[:, CHUNK_B - 1:CHUNK_B], (n_pairs, CHUNK_B, LANE)),
         jnp.broadcast_to(gc[:, PAIR - 1:PAIR], (n_pairs, CHUNK_B, LANE))], axis=1)
    e_g = jnp.exp(gc)
    e_dec = jnp.exp(g_last - gc)
    gamma = jnp.exp(g_last)

    ii = lax.broadcasted_iota(jnp.int32, (PAIR, PAIR), 0)
    jj = lax.broadcasted_iota(jnp.int32, (PAIR, PAIR), 1)
    same_chunk = (ii ^ jj) < CHUNK_B
    tril = jnp.logical_and(same_chunk, ii >= jj)
    strict = jnp.logical_and(same_chunk, ii > jj)
    eye = jnp.where(ii == jj, 1.0, 0.0)

    kb = k * beta
    q3 = q.reshape(n_pairs, PAIR, LANE)
    k3 = k.reshape(n_pairs, PAIR, LANE)
    kb3 = kb.reshape(n_pairs, PAIR, LANE)
    vb3 = (v * beta).reshape(n_pairs, PAIR, LANE)

    state = state_ref[...]
    outs = []
    for pi in range(n_pairs):
        gcp = gc[pi]
        decay = jnp.where(tril, jnp.exp(jnp.minimum(gcp - gcp.T, 0.0)), 0.0)
        lhs = jnp.concatenate([kb3[pi], q3[pi]], axis=0).astype(BF16)
        kq = _dot_nt(lhs, k3[pi].astype(BF16))
        a_mat = jnp.where(strict, kq[0:PAIR] * decay, 0.0)
        attn = kq[PAIR:2 * PAIR] * decay
        inv = eye - jnp.where(jnp.logical_and((ii ^ jj) < 2, (ii & 1) != 0), a_mat, 0.0)
        s = 2
        while s < CHUNK_B:
            c_mask = jnp.logical_and((ii ^ jj) < 2 * s, jnp.logical_and((ii & s) != 0, (jj & s) == 0))
            c_s = jnp.where(c_mask, a_mat, 0.0).astype(BF16)
            inv_b = inv.astype(BF16)
            inv = inv - _dot(inv_b, _dot(c_s, inv_b).astype(BF16))
            s *= 2
        rhs = jnp.concatenate([kb3[pi] * e_g[pi], vb3[pi]], axis=1).astype(BF16)
        wu = _dot(inv.astype(BF16), rhs).astype(BF16)
        aw = _dot(attn.astype(BF16), wu)
        q_eff = q3[pi] * e_g[pi] - aw[:, 0:LANE]
        o_loc = aw[:, LANE:2 * LANE]
        k_dec_t = (k3[pi] * e_dec[pi]).T
        tcol = lax.broadcasted_iota(jnp.int32, (LANE, PAIR), 1)
        for ci in range(2):
            lo = ci * CHUNK_B
            in_chunk = jnp.logical_and(tcol >= lo, tcol < lo + CHUNK_B)
            pn = _dot(jnp.where(in_chunk, k_dec_t, 0.0).astype(BF16), wu)
            lhs2 = jnp.concatenate([pn[:, 0:LANE], q_eff[lo:lo + CHUNK_B]], axis=0).astype(BF16)
            r = _dot(lhs2, state.astype(BF16))
            outs.append(r[LANE:LANE + CHUNK_B] + o_loc[lo:lo + CHUNK_B])
            state = state * gamma[pi, lo:lo + 1] + pn[:, LANE:2 * LANE] - r[0:LANE]
    state_ref[...] = state
    o = jnp.concatenate(outs, axis=0)
    o_ref[0] = (_rmsnorm(o, on_ref[...]) * _silu(z_ref[0])).astype(BF16)


def _gdn(proj_b, small, conv_w, a_log, dt_bias, onorm):
    bsz, seq, _ = proj_b.shape
    rows = GDN_ROWS
    nh = N_HEADS
    return pl.pallas_call(
        _gdn_kernel,
        grid=(bsz, nh, seq // rows),
        in_specs=[
            pl.BlockSpec(memory_space=pltpu.SMEM),
            pl.BlockSpec(memory_space=pltpu.SMEM),
            pl.BlockSpec((1, rows, HEAD_W), lambda b, h, s: (b, s, h)),
            pl.BlockSpec((1, rows, HEAD_W), lambda b, h, s: (b, s, nh + h)),
            pl.BlockSpec((1, rows, HEAD_W), lambda b, h, s: (b, s, 2 * nh + h)),
            pl.BlockSpec((1, rows, HEAD_W), lambda b, h, s: (b, s, 3 * nh + h)),
            pl.BlockSpec((1, rows, LANE), lambda b, h, s: (b, s, 0)),
            pl.BlockSpec((CONV_B, HEAD_W), lambda b, h, s: (0, h)),
            pl.BlockSpec((CONV_B, HEAD_W), lambda b, h, s: (0, nh + h)),
            pl.BlockSpec((CONV_B, HEAD_W), lambda b, h, s: (0, 2 * nh + h)),
            pl.BlockSpec((1, HEAD_W), lambda b, h, s: (0, 0)),
        ],
        out_specs=pl.BlockSpec((1, rows, HEAD_W), lambda b, h, s: (b, s, h)),
        out_shape=jax.ShapeDtypeStruct((bsz, seq, D_MODEL), BF16),
        scratch_shapes=[
            pltpu.VMEM((DK_B, HEAD_W), F32),
            pltpu.VMEM((3, HALO, HEAD_W), F32),
        ],
        compiler_params=_params(("parallel", "parallel", "arbitrary")),
        name="gated_delta_rule",
    )(a_log, dt_bias, proj_b, proj_b, proj_b, proj_b, small, conv_w, conv_w, conv_w, onorm)


def _merge_kernel(x_ref, oa_ref, ob_ref, ga_ref, gb_ref, wa_ref, wb_ref, wo_ref, o_ref):
    ya = _dot(oa_ref[...], wa_ref[...])
    yb = _dot(ob_ref[...], wb_ref[...])
    mixed = _sigmoid(ga_ref[...]) * ya + _sigmoid(gb_ref[...]) * yb
    o_ref[...] = x_ref[...] + _dot(mixed.astype(BF16), wo_ref[...])


def _merge(x2, oa, ob, proj_b, wa, wb, wo):
    rows = x2.shape[0]
    gate_block = 4 * D_MODEL // D_MODEL
    row_spec = pl.BlockSpec((MERGE_ROWS, D_MODEL), lambda i: (i, 0))
    return pl.pallas_call(
        _merge_kernel,
        grid=(rows // MERGE_ROWS,),
        in_specs=[
            row_spec, row_spec, row_spec,
            pl.BlockSpec((MERGE_ROWS, D_MODEL), lambda i: (i, gate_block)),
            pl.BlockSpec((MERGE_ROWS, D_MODEL), lambda i: (i, gate_block + 1)),
            _resident((D_MODEL, D_MODEL)), _resident((D_MODEL, D_MODEL)), _resident((D_MODEL, D_MODEL)),
        ],
        out_specs=row_spec,
        out_shape=jax.ShapeDtypeStruct((rows, D_MODEL), F32),
        compiler_params=_params(("parallel",)),
        name="branch_merge",
    )(x2, oa, ob, proj_b, proj_b, wa, wb, wo)


def _ffn_ple_kernel(x_ref, halo_ref, p_ref, fg_ref, wup_ref, cw_ref, cb_ref, wdn_ref, pg_ref, wg_ref, wp_ref,
                    o_ref, act_ref, *, seq):
    rows = x_ref.shape[0]
    x = x_ref[...]
    fg = fg_ref[...]
    keep = jnp.where((pl.program_id(0) * rows) % seq == 0, 0.0, 1.0)
    h_ext = jnp.concatenate([_rmsnorm(halo_ref[...], fg) * keep, _rmsnorm(x, fg)], axis=0).astype(BF16)
    off = HALO - (CONV_FF - 1)

    def conv(u, lo):
        y = cb_ref[:, lo:lo + FFN_COLS]
        for tap in range(CONV_FF):
            y = y + cw_ref[tap:tap + 1, lo:lo + FFN_COLS] * u[off + tap:off + tap + rows]
        return y

    for c in range(D_FF // FFN_COLS):
        lo = c * FFN_COLS
        yg = conv(_dot(h_ext, wup_ref[:, lo:lo + FFN_COLS]), lo)
        yv = conv(_dot(h_ext, wup_ref[:, D_FF + lo:D_FF + lo + FFN_COLS]), D_FF + lo)
        act_ref[:, lo:lo + FFN_COLS] = (_silu(yg) * yv).astype(BF16)

    x1 = x + _dot(act_ref[...], wdn_ref[...])
    h2 = _rmsnorm(x1, pg_ref[...]).astype(BF16)
    gate = _sigmoid(_dot(h2, wg_ref[...]))
    o_ref[...] = x1 + gate * _dot(p_ref[...].astype(BF16), wp_ref[...])


def _ffn_ple(x2, p2, fg, wup, cw, cb, wdn, pg, wg, wp, seq):
    rows = x2.shape[0]
    assert seq % FFN_ROWS == 0 and D_FF % FFN_COLS == 0
    halo_blocks = FFN_ROWS // HALO
    return pl.pallas_call(
        functools.partial(_ffn_ple_kernel, seq=seq),
        grid=(rows // FFN_ROWS,),
        in_specs=[
            pl.BlockSpec((FFN_ROWS, D_MODEL), lambda i: (i, 0)),
            pl.BlockSpec((HALO, D_MODEL), lambda i: (jnp.maximum(i * halo_blocks - 1, 0), 0)),
            pl.BlockSpec((FFN_ROWS, D_PLE), lambda i: (i, 0)),
            _resident((1, D_MODEL)),
            _resident((D_MODEL, 2 * D_FF)),
            _resident((CONV_FF, 2 * D_FF)),
            _resident((1, 2 * D_FF)),
            _resident((D_FF, D_MODEL)),
            _resident((1, D_MODEL)),
            _resident((D_MODEL, D_MODEL)),
            _resident((D_PLE, D_MODEL)),
        ],
        out_specs=pl.BlockSpec((FFN_ROWS, D_MODEL), lambda i: (i, 0)),
        out_shape=jax.ShapeDtypeStruct((rows, D_MODEL), F32),
        scratch_shapes=[pltpu.VMEM((FFN_ROWS, D_FF), BF16)],
        compiler_params=_params(("arbitrary",)),
        name="conv_ffn_ple",
    )(x2, x2, p2, fg, wup, cw, cb, wdn, pg, wg, wp)


def kernel(x, p, w_in, attn_norm, qnorm_a, knorm_a, lam_a, subln_a, rel_bias, conv_b, a_log_b, dt_bias_b, onorm_b, w_br_a, w_br_b, w_out, ffn_norm, w_up, conv_ff, conv_ff_bias, w_down, ple_norm, w_ple_gate, w_ple_proj):
    bsz, seq, _ = x.shape
    depth = w_in.shape[0]
    rows = bsz * seq
    assert rows % PROJ_ROWS == 0 and seq % ATT_BLOCK == 0 and seq % GDN_ROWS == 0 and rows % MERGE_ROWS == 0

    n_a = 3 * D_MODEL
    n_b = 4 * D_MODEL
    w_in_b = w_in.astype(BF16)
    rel_bias_t = rel_bias.astype(F32).T
    bias_tiles = _bias_tiles(rel_bias_t)
    n_maps = D_MODEL // HEAD_DIM_A

    x2 = x.reshape(rows, D_MODEL).astype(F32)
    for i in range(depth):
        lam_init = 0.8 - 0.6 * math.exp(-0.3 * i)
        w_a = w_in_b[i, :, 0:n_a]
        w_b = jnp.concatenate([w_in_b[i, :, n_a:n_a + n_b], w_in_b[i, :, n_a + n_b + 2 * N_HEADS:]], axis=1)
        w_small = jnp.pad(w_in_b[i, :, n_a + n_b:n_a + n_b + 2 * N_HEADS], ((0, 0), (0, LANE - 2 * N_HEADS)))
        gains = jnp.stack([
            jnp.tile(qnorm_a[i].astype(F32), n_maps) * (HEAD_DIM_A ** -0.5),
            jnp.tile(knorm_a[i].astype(F32), n_maps),
            jnp.ones((D_MODEL,), F32),
        ]).reshape(3, 1, D_MODEL)
        g_attn = attn_norm[i].astype(F32).reshape(1, D_MODEL)

        qkv = _attn_proj(x2, g_attn, w_a, gains)
        proj_b, small = _gdn_proj(x2, g_attn, w_b, w_small)

        oa = _attention(qkv.reshape(bsz, seq, n_a), bias_tiles, rel_bias_t, lam_a[i].astype(F32),
                        subln_a[i].astype(F32).reshape(1, HEAD_W), jnp.full((1,), lam_init, F32))
        ob = _gdn(proj_b.reshape(bsz, seq, -1), small.reshape(bsz, seq, LANE), conv_b[i].astype(F32),
                  a_log_b[i].astype(F32), dt_bias_b[i].astype(F32), onorm_b[i].astype(F32).reshape(1, HEAD_W))

        x2 = _merge(x2, oa.reshape(rows, D_MODEL), ob.reshape(rows, D_MODEL), proj_b,
                    w_br_a[i].astype(BF16), w_br_b[i].astype(BF16), w_out[i].astype(BF16))

        x2 = _ffn_ple(x2, p[i].reshape(rows, D_PLE), ffn_norm[i].astype(F32).reshape(1, D_MODEL),
                      w_up[i].astype(BF16), conv_ff[i].astype(F32), conv_ff_bias[i].astype(F32).reshape(1, 2 * D_FF),
                      w_down[i].astype(BF16), ple_norm[i].astype(F32).reshape(1, D_MODEL),
                      w_ple_gate[i].astype(BF16), w_ple_proj[i].astype(BF16), seq)
    return x2.reshape(bsz, seq, D_MODEL).astype(x.dtype)
```

```python
import functools
import math

import numpy as np
import jax
import jax.numpy as jnp
from jax import lax
from jax.experimental import pallas as pl
from jax.experimental.pallas import tpu as pltpu

F32 = jnp.float32
BF16 = jnp.bfloat16

D_MODEL = 1024
N_HEADS = 8
HEAD_W = 128
HEAD_DIM_A = 64
DK_B = 128
CONV_B = 4
CHUNK_B = 64
D_FF = 2816
CONV_FF = 3
D_PLE = 256
N_BUCKETS = 32
MAX_DISTANCE = 128
EPS = 1e-6
NEG_INF = -1e30

LANE = 128
SUBLANE = 8
VMEM_LIMIT_BYTES = 56 * 1024 * 1024

PROJ_ROWS = 1024
PROJ_COLS = 1024
ATT_BLOCK = 512
GDN_CHUNK = 128
MERGE_ROWS = 512
FFN_ROWS = 512
FFN_COLS = 256
HALO = SUBLANE
PROJ_HALO = 2 * SUBLANE


def _bucket_thresholds():
    n = np.arange(0, 4 * MAX_DISTANCE)
    max_exact = N_BUCKETS // 2
    nf = np.maximum(n, 1).astype(np.float32)
    large = max_exact + (np.log(nf / np.float32(max_exact)) / np.float32(math.log(MAX_DISTANCE / max_exact))
                         * np.float32(N_BUCKETS - max_exact)).astype(np.int32)
    bucket = np.where(n < max_exact, n, np.minimum(large, N_BUCKETS - 1))
    return [int(np.argmax(bucket >= b)) for b in range(1, N_BUCKETS)]


BUCKET_LO = _bucket_thresholds()


def _params(semantics):
    return pltpu.CompilerParams(dimension_semantics=semantics, vmem_limit_bytes=VMEM_LIMIT_BYTES)


def _resident(shape):
    zeros = (0,) * len(shape)
    return pl.BlockSpec(shape, lambda *_: zeros, pipeline_mode=pl.Buffered(1))


def _rmsnorm(x, g):
    ms = jnp.mean(x * x, axis=-1, keepdims=True)
    return x * lax.rsqrt(ms + EPS) * g


def _sigmoid(x):
    return 1.0 / (1.0 + jnp.exp(-x))


def _silu(x):
    return x * _sigmoid(x)


def _dot(a, b):
    return jnp.dot(a, b, preferred_element_type=F32)


def _dot_nt(a, b):
    return lax.dot_general(a, b, (((1,), (1,)), ((), ())), preferred_element_type=F32)


def _dot_tn(a, b):
    return lax.dot_general(a, b, (((0,), (0,)), ((), ())), preferred_element_type=F32)


def _attn_proj_kernel(x_ref, g_ref, w_ref, gain_ref, gsum_ref, gexp_ref, o_ref, h_ref):
    j = pl.program_id(1)

    @pl.when(j == 0)
    def _():
        h_ref[...] = _rmsnorm(x_ref[...], g_ref[...]).astype(BF16)

    y = _dot(h_ref[...], w_ref[...])

    @pl.when(j < 2)
    def _():
        ss = _dot((y * y).astype(BF16), gsum_ref[...])
        r = lax.rsqrt(ss * (1.0 / HEAD_DIM_A) + EPS)
        r_hi = r.astype(BF16)
        r_lo = (r - r_hi.astype(F32)).astype(BF16)
        rb = _dot(jnp.concatenate([r_hi, r_lo], axis=1), gexp_ref[...])
        o_ref[...] = (y * rb * gain_ref[...]).astype(BF16)

    @pl.when(j == 2)
    def _():
        o_ref[...] = y.astype(BF16)


def _attn_proj(x2, g, w, gains):
    rows = x2.shape[0]
    n_groups = D_MODEL // HEAD_DIM_A
    col = np.arange(D_MODEL) // HEAD_DIM_A
    gsum = (col[:, None] == np.arange(LANE)[None, :]).astype(np.float32)
    gexp = np.concatenate([gsum.T, gsum.T], axis=0)
    assert n_groups <= LANE
    return pl.pallas_call(
        _attn_proj_kernel,
        grid=(rows // PROJ_ROWS, 3),
        in_specs=[
            pl.BlockSpec((PROJ_ROWS, D_MODEL), lambda i, j: (i, 0)),
            pl.BlockSpec((1, D_MODEL), lambda i, j: (0, 0)),
            pl.BlockSpec((D_MODEL, PROJ_COLS), lambda i, j: (0, j)),
            pl.BlockSpec((None, 1, PROJ_COLS), lambda i, j: (j, 0, 0)),
            pl.BlockSpec((D_MODEL, LANE), lambda i, j: (0, 0)),
            pl.BlockSpec((2 * LANE, D_MODEL), lambda i, j: (0, 0)),
        ],
        out_specs=pl.BlockSpec((PROJ_ROWS, PROJ_COLS), lambda i, j: (i, j)),
        out_shape=jax.ShapeDtypeStruct((rows, 3 * D_MODEL), BF16),
        scratch_shapes=[pltpu.VMEM((PROJ_ROWS, D_MODEL), BF16)],
        compiler_params=_params(("parallel", "arbitrary")),
        name="attn_proj",
    )(x2, g, w, gains, jnp.asarray(gsum, BF16), jnp.asarray(gexp, BF16))


def _gdn_proj_kernel(x_ref, halo_ref, g_ref, w_ref, wsm_ref, cw_ref, o_ref, osm_ref, h_ref, *, seq):
    i = pl.program_id(0)
    j = pl.program_id(1)
    rows = x_ref.shape[0]

    @pl.when(j == 0)
    def _():
        g = g_ref[...]
        keep = jnp.where((i * rows) % seq == 0, 0.0, 1.0)
        h_ref[0:PROJ_HALO, :] = (_rmsnorm(halo_ref[...], g) * keep).astype(BF16)
        h = _rmsnorm(x_ref[...], g).astype(BF16)
        h_ref[PROJ_HALO:, :] = h
        osm_ref[...] = _dot(h, wsm_ref[...])

    @pl.when(j < 3)
    def _():
        u = _dot(h_ref[...], w_ref[...])
        off = PROJ_HALO - (CONV_B - 1)
        scale = jnp.where(j == 0, DK_B ** -0.5, 1.0)
        for hh in range(PROJ_COLS // HEAD_W):
            lo = hh * HEAD_W
            y = cw_ref[0:1, lo:lo + HEAD_W] * u[off:off + rows, lo:lo + HEAD_W]
            for tap in range(1, CONV_B):
                y = y + cw_ref[tap:tap + 1, lo:lo + HEAD_W] * u[off + tap:off + tap + rows, lo:lo + HEAD_W]
            y = _silu(y)
            r = lax.rsqrt(jnp.sum(y * y, axis=-1, keepdims=True) + EPS) * scale
            o_ref[:, lo:lo + HEAD_W] = (y * jnp.where(j == 2, 1.0, r)).astype(BF16)

    @pl.when(j >= 3)
    def _():
        o_ref[...] = _dot(h_ref[PROJ_HALO:, :], w_ref[...]).astype(BF16)


def _gdn_proj(x2, g, w, wsm, conv_w, seq):
    rows = x2.shape[0]
    n_cols = w.shape[1]
    assert seq % PROJ_ROWS == 0 and PROJ_ROWS % PROJ_HALO == 0
    halo_blocks = PROJ_ROWS // PROJ_HALO
    return pl.pallas_call(
        functools.partial(_gdn_proj_kernel, seq=seq),
        grid=(rows // PROJ_ROWS, n_cols // PROJ_COLS),
        in_specs=[
            pl.BlockSpec((PROJ_ROWS, D_MODEL), lambda i, j: (i, 0)),
            pl.BlockSpec((PROJ_HALO, D_MODEL), lambda i, j: (jnp.maximum(i * halo_blocks - 1, 0), 0)),
            pl.BlockSpec((1, D_MODEL), lambda i, j: (0, 0)),
            pl.BlockSpec((D_MODEL, PROJ_COLS), lambda i, j: (0, j)),
            pl.BlockSpec((D_MODEL, LANE), lambda i, j: (0, 0)),
            pl.BlockSpec((CONV_B, PROJ_COLS), lambda i, j: (0, jnp.minimum(j, 2))),
        ],
        out_specs=[
            pl.BlockSpec((PROJ_ROWS, PROJ_COLS), lambda i, j: (i, j)),
            pl.BlockSpec((PROJ_ROWS, LANE), lambda i, j: (i, 0)),
        ],
        out_shape=[
            jax.ShapeDtypeStruct((rows, n_cols), BF16),
            jax.ShapeDtypeStruct((rows, LANE), F32),
        ],
        scratch_shapes=[pltpu.VMEM((PROJ_HALO + PROJ_ROWS, D_MODEL), BF16)],
        compiler_params=_params(("parallel", "arbitrary")),
        name="gdn_proj",
    )(x2, x2, g, w, wsm, conv_w)


def _bias_tiles_kernel(tab_ref, o_ref):
    h = pl.program_id(0)
    t = o_ref.shape[2]
    row = lax.broadcasted_iota(jnp.int32, (t, t), 0)
    col = lax.broadcasted_iota(jnp.int32, (t, t), 1)
    for idx in range(2):
        rel = row - col + idx * t
        b = jnp.full((t, t), tab_ref[h, 0], F32)
        for bk in range(1, N_BUCKETS):
            b = jnp.where(rel >= BUCKET_LO[bk - 1], tab_ref[h, bk], b)
        if idx == 0:
            b = jnp.where(rel >= 0, b, NEG_INF)
        o_ref[0, idx] = b


def _bias_tiles(rel_bias_t):
    t = ATT_BLOCK
    return pl.pallas_call(
        _bias_tiles_kernel,
        grid=(N_HEADS,),
        in_specs=[pl.BlockSpec(memory_space=pltpu.SMEM)],
        out_specs=pl.BlockSpec((1, 2, t, t), lambda h: (h, 0, 0, 0)),
        out_shape=jax.ShapeDtypeStruct((N_HEADS, 2, t, t), F32),
        compiler_params=_params(("parallel",)),
        name="bias_tiles",
    )(rel_bias_t)


def _attn_kernel(qi_tab, kj_tab, q_ref, k_ref, v_ref, bias_ref, tab_ref, lam_ref, sg_ref, li_ref,
                 o_ref, qs_ref, m_ref, l_ref, acc_ref):
    h = pl.program_id(0)
    step = pl.program_id(2)
    qi = qi_tab[step]
    kj = kj_tab[step]
    t = q_ref.shape[1]

    @pl.when(kj == 0)
    def _():
        q = q_ref[0]
        lane = lax.broadcasted_iota(jnp.int32, q.shape, 1)
        zero = jnp.zeros_like(q)
        qs_ref[0:t, :] = jnp.where(lane < HEAD_DIM_A, q, zero)
        qs_ref[t:2 * t, :] = jnp.where(lane >= HEAD_DIM_A, q, zero)
        m_ref[...] = jnp.full(m_ref.shape, NEG_INF, F32)
        l_ref[...] = jnp.zeros(l_ref.shape, F32)
        acc_ref[...] = jnp.zeros(acc_ref.shape, F32)

    s = _dot_nt(qs_ref[...], k_ref[0])

    def update(s2):
        m_prev = m_ref[...]
        m_new = jnp.maximum(m_prev, jnp.max(s2, axis=-1, keepdims=True))
        alpha = jnp.exp(m_prev - m_new)
        p = jnp.exp(s2 - m_new)
        l_ref[...] = alpha * l_ref[...] + jnp.sum(p, axis=-1, keepdims=True)
        acc_ref[...] = alpha * acc_ref[...] + _dot(p.astype(BF16), v_ref[0])
        m_ref[...] = m_new

    delta = qi - kj

    @pl.when(delta >= 2)
    def _():
        update(s + tab_ref[h, N_BUCKETS - 1])

    @pl.when(delta == 1)
    def _():
        b = bias_ref[0, 1]
        update(s + jnp.concatenate([b, b], axis=0))

    @pl.when(delta == 0)
    def _():
        b = bias_ref[0, 0]
        update(s + jnp.concatenate([b, b], axis=0))
        acc = acc_ref[...]
        l = l_ref[...]
        o0 = acc[0:t] / l[0:t]
        o1 = acc[t:2 * t] / l[t:2 * t]
        lq = lam_ref[...]
        lam_init = li_ref[0]
        lam = (jnp.exp(jnp.sum(lq[0:1] * lq[1:2], axis=-1, keepdims=True))
               - jnp.exp(jnp.sum(lq[2:3] * lq[3:4], axis=-1, keepdims=True)) + lam_init)
        o = o0 - lam * o1
        o_ref[0] = (_rmsnorm(o, sg_ref[...]) * (1.0 - lam_init)).astype(BF16)


def _attention(qkv, bias_tiles, rel_bias_t, lam_a, subln, lam_init):
    bsz, seq, _ = qkv.shape
    t = ATT_BLOCK
    nq = seq // t
    assert t + 1 >= BUCKET_LO[-1], "tiles two or more blocks below the diagonal must sit in the last bucket"
    pairs = [(qi, kj) for qi in range(nq) for kj in range(qi + 1)]
    qi_tab = jnp.asarray([p[0] for p in pairs], jnp.int32)
    kj_tab = jnp.asarray([p[1] for p in pairs], jnp.int32)
    grid_spec = pltpu.PrefetchScalarGridSpec(
        num_scalar_prefetch=2,
        grid=(N_HEADS, bsz, len(pairs)),
        in_specs=[
            pl.BlockSpec((1, t, HEAD_W), lambda h, b, s, qt, kt: (b, qt[s], h)),
            pl.BlockSpec((1, t, HEAD_W), lambda h, b, s, qt, kt: (b, kt[s], N_HEADS + h)),
            pl.BlockSpec((1, t, HEAD_W), lambda h, b, s, qt, kt: (b, kt[s], 2 * N_HEADS + h)),
            pl.BlockSpec((1, 2, t, t), lambda h, b, s, qt, kt: (h, 0, 0, 0)),
            pl.BlockSpec(memory_space=pltpu.SMEM),
            pl.BlockSpec((4, HEAD_DIM_A), lambda h, b, s, qt, kt: (0, 0)),
            pl.BlockSpec((1, HEAD_W), lambda h, b, s, qt, kt: (0, 0)),
            pl.BlockSpec(memory_space=pltpu.SMEM),
        ],
        out_specs=pl.BlockSpec((1, t, HEAD_W), lambda h, b, s, qt, kt: (b, qt[s], h)),
        scratch_shapes=[
            pltpu.VMEM((2 * t, HEAD_W), BF16),
            pltpu.VMEM((2 * t, 1), F32),
            pltpu.VMEM((2 * t, 1), F32),
            pltpu.VMEM((2 * t, HEAD_W), F32),
        ],
    )
    return pl.pallas_call(
        _attn_kernel,
        grid_spec=grid_spec,
        out_shape=jax.ShapeDtypeStruct((bsz, seq, D_MODEL), BF16),
        compiler_params=_params(("parallel", "parallel", "arbitrary")),
        name="diff_attention",
    )(qi_tab, kj_tab, qkv, qkv, qkv, bias_tiles, rel_bias_t, lam_a, subln, lam_init)


def _split(x, n_pieces):
    pieces = []
    for _ in range(n_pieces):
        piece = x.astype(BF16)
        pieces.append(piece)
        x = x - piece.astype(F32)
    return pieces


def _gdn_kernel(q_ref, k_ref, v_ref, z_ref, sm_ref, alog_ref, dtb_ref, on_ref, tril_ref, sel_ref, lvl_ref,
                o_ref, state_ref):
    c = GDN_CHUNK

    @pl.when(pl.program_id(1) == 0)
    def _():
        state_ref[...] = jnp.zeros(state_ref.shape, F32)

    sm = sm_ref[0]
    beta_all = _sigmoid(sm)
    xa = sm + dtb_ref[...]
    softplus = jnp.maximum(xa, 0.0) + jnp.log1p(jnp.exp(-jnp.abs(xa)))
    g_all = -jnp.exp(alog_ref[...]) * softplus
    gc_all = _dot(tril_ref[...], jnp.concatenate(_split(g_all, 3), axis=0))
    eg_all = jnp.exp(gc_all)
    ed_all = jnp.exp(gc_all[c - 1:c, :] - gc_all)
    beta_p = jnp.concatenate(_split(beta_all, 2), axis=1)
    gc_p = jnp.concatenate(_split(gc_all, 3), axis=1)
    eg_p = jnp.concatenate(_split(eg_all, 2), axis=1)
    ed_p = jnp.concatenate(_split(ed_all, 2), axis=1)

    ii = lax.broadcasted_iota(jnp.int32, (c, c), 0)
    jj = lax.broadcasted_iota(jnp.int32, (c, c), 1)
    tril = ii >= jj
    strict = ii > jj
    n_levels = lvl_ref.shape[0] - 1

    heads = range(N_HEADS)
    cols = [slice(h * HEAD_W, (h + 1) * HEAD_W) for h in heads]
    beta = [_dot(beta_p, sel_ref[h, 0:2 * LANE]) for h in heads]
    gcr = [_dot(gc_p, sel_ref[N_HEADS + h]) for h in heads]
    eg = [_dot(eg_p, sel_ref[N_HEADS + h, 0:2 * LANE]) for h in heads]
    ed = [_dot(ed_p, sel_ref[N_HEADS + h, 0:2 * LANE]) for h in heads]
    k = [k_ref[0, :, cols[h]] for h in heads]
    kq = [_dot_nt(jnp.concatenate([k[h], q_ref[0, :, cols[h]]], axis=0), k[h]) for h in heads]
    decay = [jnp.where(tril, jnp.exp(gcr[h] - gcr[h].T), 0.0) for h in heads]
    a_b = [jnp.where(strict, kq[h][0:c] * (decay[h] * beta[h]), 0.0).astype(BF16) for h in heads]
    attn_b = [(kq[h][c:2 * c] * decay[h]).astype(BF16) for h in heads]
    inv = [lvl_ref[0] - a_b[h] * lvl_ref[1] for h in heads]
    for lv in range(2, n_levels + 1):
        x = [_dot(a_b[h] * lvl_ref[lv], inv[h]).astype(BF16) for h in heads]
        inv = [inv[h] - _dot(inv[h], x[h]).astype(BF16) for h in heads]
    rhs = [jnp.concatenate([k[h] * (beta[h] * eg[h]).astype(BF16),
                            v_ref[0, :, cols[h]] * beta[h].astype(BF16)], axis=1) for h in heads]
    wu = [_dot(inv[h], rhs[h]).astype(BF16) for h in heads]
    aw = [_dot(attn_b[h], wu[h]) for h in heads]
    pn = [_dot_tn(k[h] * ed[h].astype(BF16), wu[h]) for h in heads]
    q_eff = [q_ref[0, :, cols[h]].astype(F32) * eg[h] - aw[h][:, 0:HEAD_W] for h in heads]
    state = [state_ref[h] for h in heads]
    r = [_dot(jnp.concatenate([pn[h][:, 0:HEAD_W], q_eff[h]], axis=0).astype(BF16), state[h].astype(BF16))
         for h in heads]
    for h in heads:
        state_ref[h] = state[h] * eg[h][c - 1:c, :] + pn[h][:, HEAD_W:2 * HEAD_W] - r[h][0:DK_B]
    for h in heads:
        o = r[h][DK_B:DK_B + c] + aw[h][:, HEAD_W:2 * HEAD_W]
        z = z_ref[0, :, cols[h]].astype(F32)
        o_ref[0, :, cols[h]] = (_rmsnorm(o, on_ref[...]) * _silu(z)).astype(BF16)


def _gdn_constants():
    c = GDN_CHUNK
    r = np.arange(c)
    tril = (r[None, :] <= r[:, None]).astype(np.float32)
    tril3 = np.concatenate([tril, tril, tril], axis=1)
    lane = np.arange(3 * LANE) % LANE
    sel = (lane[None, :, None] == np.arange(2 * N_HEADS)[:, None, None]).astype(np.float32)
    sel = np.broadcast_to(sel, (2 * N_HEADS, 3 * LANE, LANE))
    levels = [np.eye(c, dtype=np.float32)]
    s = 1
    while s < c:
        same = (r[:, None] ^ r[None, :]) < 2 * s
        levels.append((same & ((r[:, None] & s) != 0) & ((r[None, :] & s) == 0)).astype(np.float32))
        s *= 2
    return jnp.asarray(tril3, BF16), jnp.asarray(sel, BF16), jnp.asarray(np.stack(levels), BF16)


def _gdn(proj_b, small, alog_row, dtb_row, onorm):
    bsz, seq, _ = proj_b.shape
    c = GDN_CHUNK
    assert c == LANE and seq % c == 0
    tril3, sel, levels = _gdn_constants()
    blk = lambda col: pl.BlockSpec((1, c, D_MODEL), lambda b, s: (b, s, col))
    return pl.pallas_call(
        _gdn_kernel,
        grid=(bsz, seq // c),
        in_specs=[
            blk(0), blk(1), blk(2), blk(3),
            pl.BlockSpec((1, c, LANE), lambda b, s: (b, s, 0)),
            pl.BlockSpec((1, LANE), lambda b, s: (0, 0)),
            pl.BlockSpec((1, LANE), lambda b, s: (0, 0)),
            pl.BlockSpec((1, HEAD_W), lambda b, s: (0, 0)),
            _resident(tril3.shape), _resident(sel.shape), _resident(levels.shape),
        ],
        out_specs=pl.BlockSpec((1, c, D_MODEL), lambda b, s: (b, s, 0)),
        out_shape=jax.ShapeDtypeStruct((bsz, seq, D_MODEL), BF16),
        scratch_shapes=[pltpu.VMEM((N_HEADS, DK_B, HEAD_W), F32)],
        compiler_params=_params(("parallel", "arbitrary")),
        name="gated_delta_rule",
    )(proj_b, proj_b, proj_b, proj_b, small, alog_row, dtb_row, onorm, tril3, sel, levels)


def _merge_kernel(x_ref, oa_ref, ob_ref, ga_ref, gb_ref, wa_ref, wb_ref, wo_ref, o_ref):
    ya = _dot(oa_ref[...], wa_ref[...])
    yb = _dot(ob_ref[...], wb_ref[...])
    mixed = _sigmoid(ga_ref[...].astype(F32)) * ya + _sigmoid(gb_ref[...].astype(F32)) * yb
    o_ref[...] = x_ref[...] + _dot(mixed.astype(BF16), wo_ref[...])


def _merge(x2, oa, ob, proj_b, wa, wb, wo):
    rows = x2.shape[0]
    gate_block = 4
    row_spec = pl.BlockSpec((MERGE_ROWS, D_MODEL), lambda i: (i, 0))
    return pl.pallas_call(
        _merge_kernel,
        grid=(rows // MERGE_ROWS,),
        in_specs=[
            row_spec, row_spec, row_spec,
            pl.BlockSpec((MERGE_ROWS, D_MODEL), lambda i: (i, gate_block)),
            pl.BlockSpec((MERGE_ROWS, D_MODEL), lambda i: (i, gate_block + 1)),
            _resident((D_MODEL, D_MODEL)), _resident((D_MODEL, D_MODEL)), _resident((D_MODEL, D_MODEL)),
        ],
        out_specs=row_spec,
        out_shape=jax.ShapeDtypeStruct((rows, D_MODEL), F32),
        compiler_params=_params(("parallel",)),
        name="branch_merge",
    )(x2, oa, ob, proj_b, proj_b, wa, wb, wo)


def _ffn_ple_kernel(x_ref, halo_ref, p_ref, fg_ref, wup_ref, cw_ref, cb_ref, wdn_ref, pg_ref, wg_ref, wp_ref,
                    o_ref, act_ref, *, seq):
    rows = x_ref.shape[0]
    x = x_ref[...]
    fg = fg_ref[...]
    keep = jnp.where((pl.program_id(0) * rows) % seq == 0, 0.0, 1.0)
    h_ext = jnp.concatenate([_rmsnorm(halo_ref[...], fg) * keep, _rmsnorm(x, fg)], axis=0).astype(BF16)
    off = HALO - (CONV_FF - 1)

    def conv(u, lo):
        y = cb_ref[:, lo:lo + FFN_COLS]
        for tap in range(CONV_FF):
            y = y + cw_ref[tap:tap + 1, lo:lo + FFN_COLS] * u[off + tap:off + tap + rows]
        return y

    for c in range(D_FF // FFN_COLS):
        lo = c * FFN_COLS
        yg = conv(_dot(h_ext, wup_ref[:, lo:lo + FFN_COLS]), lo)
        yv = conv(_dot(h_ext, wup_ref[:, D_FF + lo:D_FF + lo + FFN_COLS]), D_FF + lo)
        act_ref[:, lo:lo + FFN_COLS] = (_silu(yg) * yv).astype(BF16)

    x1 = x + _dot(act_ref[...], wdn_ref[...])
    h2 = _rmsnorm(x1, pg_ref[...]).astype(BF16)
    gate = _sigmoid(_dot(h2, wg_ref[...]))
    o_ref[...] = x1 + gate * _dot(p_ref[...].astype(BF16), wp_ref[...])


def _ffn_ple(x2, p2, fg, wup, cw, cb, wdn, pg, wg, wp, seq):
    rows = x2.shape[0]
    assert seq % FFN_ROWS == 0 and D_FF % FFN_COLS == 0
    halo_blocks = FFN_ROWS // HALO
    return pl.pallas_call(
        functools.partial(_ffn_ple_kernel, seq=seq),
        grid=(rows // FFN_ROWS,),
        in_specs=[
            pl.BlockSpec((FFN_ROWS, D_MODEL), lambda i: (i, 0)),
            pl.BlockSpec((HALO, D_MODEL), lambda i: (jnp.maximum(i * halo_blocks - 1, 0), 0)),
            pl.BlockSpec((FFN_ROWS, D_PLE), lambda i: (i, 0)),
            _resident((1, D_MODEL)),
            _resident((D_MODEL, 2 * D_FF)),
            _resident((CONV_FF, 2 * D_FF)),
            _resident((1, 2 * D_FF)),
            _resident((D_FF, D_MODEL)),
            _resident((1, D_MODEL)),
            _resident((D_MODEL, D_MODEL)),
            _resident((D_PLE, D_MODEL)),
        ],
        out_specs=pl.BlockSpec((FFN_ROWS, D_MODEL), lambda i: (i, 0)),
        out_shape=jax.ShapeDtypeStruct((rows, D_MODEL), F32),
        scratch_shapes=[pltpu.VMEM((FFN_ROWS, D_FF), BF16)],
        compiler_params=_params(("arbitrary",)),
        name="conv_ffn_ple",
    )(x2, x2, p2, fg, wup, cw, cb, wdn, pg, wg, wp)


def kernel(x, p, w_in, attn_norm, qnorm_a, knorm_a, lam_a, subln_a, rel_bias, conv_b, a_log_b, dt_bias_b, onorm_b, w_br_a, w_br_b, w_out, ffn_norm, w_up, conv_ff, conv_ff_bias, w_down, ple_norm, w_ple_gate, w_ple_proj):
    bsz, seq, _ = x.shape
    depth = w_in.shape[0]
    rows = bsz * seq
    assert rows % PROJ_ROWS == 0 and seq % ATT_BLOCK == 0 and rows % MERGE_ROWS == 0

    n_a = 3 * D_MODEL
    n_b = 4 * D_MODEL
    w_in_b = w_in.astype(BF16)
    rel_bias_t = rel_bias.astype(F32).T
    bias_tiles = _bias_tiles(rel_bias_t)
    n_maps = D_MODEL // HEAD_DIM_A

    x2 = x.reshape(rows, D_MODEL).astype(F32)
    for i in range(depth):
        lam_init = 0.8 - 0.6 * math.exp(-0.3 * i)
        w_a = w_in_b[i, :, 0:n_a]
        w_b = jnp.concatenate([w_in_b[i, :, n_a:n_a + n_b], w_in_b[i, :, n_a + n_b + 2 * N_HEADS:]], axis=1)
        w_small = jnp.pad(w_in_b[i, :, n_a + n_b:n_a + n_b + 2 * N_HEADS], ((0, 0), (0, LANE - 2 * N_HEADS)))
        gains = jnp.stack([
            jnp.tile(qnorm_a[i].astype(F32), n_maps) * (HEAD_DIM_A ** -0.5),
            jnp.tile(knorm_a[i].astype(F32), n_maps),
            jnp.ones((D_MODEL,), F32),
        ]).reshape(3, 1, D_MODEL)
        g_attn = attn_norm[i].astype(F32).reshape(1, D_MODEL)

        qkv = _attn_proj(x2, g_attn, w_a, gains)
        proj_b, small = _gdn_proj(x2, g_attn, w_b, w_small, conv_b[i].astype(F32), seq)

        oa = _attention(qkv.reshape(bsz, seq, n_a), bias_tiles, rel_bias_t, lam_a[i].astype(F32),
                        subln_a[i].astype(F32).reshape(1, HEAD_W), jnp.full((1,), lam_init, F32))
        head_lanes = (N_HEADS, LANE - 2 * N_HEADS)
        ob = _gdn(proj_b.reshape(bsz, seq, -1), small.reshape(bsz, seq, LANE),
                  jnp.pad(a_log_b[i].astype(F32), head_lanes).reshape(1, LANE),
                  jnp.pad(dt_bias_b[i].astype(F32), head_lanes).reshape(1, LANE),
                  onorm_b[i].astype(F32).reshape(1, HEAD_W))

        x2 = _merge(x2, oa.reshape(rows, D_MODEL), ob.reshape(rows, D_MODEL), proj_b,
                    w_br_a[i].astype(BF16), w_br_b[i].astype(BF16), w_out[i].astype(BF16))

        x2 = _ffn_ple(x2, p[i].reshape(rows, D_PLE), ffn_norm[i].astype(F32).reshape(1, D_MODEL),
                      w_up[i].astype(BF16), conv_ff[i].astype(F32), conv_ff_bias[i].astype(F32).reshape(1, 2 * D_FF),
                      w_down[i].astype(BF16), ple_norm[i].astype(F32).reshape(1, D_MODEL),
                      w_ple_gate[i].astype(BF16), w_ple_proj[i].astype(BF16), seq)
    return x2.reshape(bsz, seq, D_MODEL).astype(x.dtype)
```

```python
import functools
import math

import numpy as np
import jax
import jax.numpy as jnp
from jax import lax
from jax.experimental import pallas as pl
from jax.experimental.pallas import tpu as pltpu

F32 = jnp.float32
BF16 = jnp.bfloat16

D_MODEL = 1024
N_HEADS = 8
HEAD_W = 128
HEAD_DIM_A = 64
DK_B = 128
CONV_B = 4
CHUNK_B = 64
D_FF = 2816
CONV_FF = 3
D_PLE = 256
N_BUCKETS = 32
MAX_DISTANCE = 128
EPS = 1e-6
NEG_INF = -1e30

LANE = 128
SUBLANE = 8
VMEM_LIMIT_BYTES = 56 * 1024 * 1024

PROJ_ROWS = 1024
PROJ_COLS = 1024
PROJ_SUB = 256
ATT_BLOCK = 512
ATT_CHUNK = 256
ATT_HEADS = 2
ATT_CORNER = 128
LOG2E = math.log2(math.e)
GDN_CHUNK = 128
MERGE_ROWS = 512
FFN_ROWS = 512
FFN_COLS = 256
HALO = SUBLANE
PROJ_HALO = 2 * SUBLANE


def _bucket_thresholds():
    n = np.arange(0, 4 * MAX_DISTANCE)
    max_exact = N_BUCKETS // 2
    nf = np.maximum(n, 1).astype(np.float32)
    large = max_exact + (np.log(nf / np.float32(max_exact)) / np.float32(math.log(MAX_DISTANCE / max_exact))
                         * np.float32(N_BUCKETS - max_exact)).astype(np.int32)
    bucket = np.where(n < max_exact, n, np.minimum(large, N_BUCKETS - 1))
    return [int(np.argmax(bucket >= b)) for b in range(1, N_BUCKETS)]


BUCKET_LO = _bucket_thresholds()


def _params(semantics):
    return pltpu.CompilerParams(dimension_semantics=semantics, vmem_limit_bytes=VMEM_LIMIT_BYTES)


def _resident(shape):
    zeros = (0,) * len(shape)
    return pl.BlockSpec(shape, lambda *_: zeros, pipeline_mode=pl.Buffered(1))


def _rmsnorm(x, g):
    ms = jnp.mean(x * x, axis=-1, keepdims=True)
    return x * lax.rsqrt(ms + EPS) * g


def _sigmoid(x):
    return 1.0 / (1.0 + jnp.exp(-x))


def _silu(x):
    return x * _sigmoid(x)


def _dot(a, b):
    return jnp.dot(a, b, preferred_element_type=F32)


def _dot_nt(a, b):
    return lax.dot_general(a, b, (((1,), (1,)), ((), ())), preferred_element_type=F32)


def _dot_tn(a, b):
    return lax.dot_general(a, b, (((0,), (0,)), ((), ())), preferred_element_type=F32)


def _attn_proj_kernel(x_ref, g_ref, w_ref, gain_ref, gsum_ref, gexp_ref, o_ref, h_ref):
    j = pl.program_id(1)

    @pl.when(j == 0)
    def _():
        h_ref[...] = _rmsnorm(x_ref[...], g_ref[...]).astype(BF16)

    y = _dot(h_ref[...], w_ref[...])

    @pl.when(j < 2)
    def _():
        ss = _dot((y * y).astype(BF16), gsum_ref[...])
        r = lax.rsqrt(ss * (1.0 / HEAD_DIM_A) + EPS)
        r_hi = r.astype(BF16)
        r_lo = (r - r_hi.astype(F32)).astype(BF16)
        rb = _dot(jnp.concatenate([r_hi, r_lo], axis=1), gexp_ref[...])
        o_ref[...] = (y * rb * gain_ref[...]).astype(BF16)

    @pl.when(j == 2)
    def _():
        o_ref[...] = y.astype(BF16)


def _attn_proj(x2, g, w, gains):
    rows = x2.shape[0]
    n_groups = D_MODEL // HEAD_DIM_A
    col = np.arange(D_MODEL) // HEAD_DIM_A
    gsum = (col[:, None] == np.arange(LANE)[None, :]).astype(np.float32)
    gexp = np.concatenate([gsum.T, gsum.T], axis=0)
    assert n_groups <= LANE
    return pl.pallas_call(
        _attn_proj_kernel,
        grid=(rows // PROJ_ROWS, 3),
        in_specs=[
            pl.BlockSpec((PROJ_ROWS, D_MODEL), lambda i, j: (i, 0)),
            pl.BlockSpec((1, D_MODEL), lambda i, j: (0, 0)),
            pl.BlockSpec((D_MODEL, PROJ_COLS), lambda i, j: (0, j)),
            pl.BlockSpec((None, 1, PROJ_COLS), lambda i, j: (j, 0, 0)),
            pl.BlockSpec((D_MODEL, LANE), lambda i, j: (0, 0)),
            pl.BlockSpec((2 * LANE, D_MODEL), lambda i, j: (0, 0)),
        ],
        out_specs=pl.BlockSpec((PROJ_ROWS, PROJ_COLS), lambda i, j: (i, j)),
        out_shape=jax.ShapeDtypeStruct((rows, 3 * D_MODEL), BF16),
        scratch_shapes=[pltpu.VMEM((PROJ_ROWS, D_MODEL), BF16)],
        compiler_params=_params(("parallel", "arbitrary")),
        name="attn_proj",
    )(x2, g, w, gains, jnp.asarray(gsum, BF16), jnp.asarray(gexp, BF16))


def _gdn_proj_kernel(x_ref, halo_ref, g_ref, w_ref, wsm_ref, cw_ref, o_ref, osm_ref, h_ref, u_ref, *, seq):
    i = pl.program_id(0)
    j = pl.program_id(1)
    rows = x_ref.shape[0]

    @pl.when(j == 0)
    def _():
        g = g_ref[...]
        keep = jnp.where((i * rows) % seq == 0, 0.0, 1.0)
        h_ref[0:PROJ_HALO, :] = (_rmsnorm(halo_ref[...], g) * keep).astype(BF16)
        h = _rmsnorm(x_ref[...], g).astype(BF16)
        h_ref[PROJ_HALO:, :] = h
        osm_ref[...] = _dot(h, wsm_ref[...])

    @pl.when(j < 3)
    def _():
        off = PROJ_HALO - (CONV_B - 1)
        scale = jnp.where(j == 0, DK_B ** -0.5, 1.0)

        def project(sub):
            cols = slice(sub * PROJ_SUB, (sub + 1) * PROJ_SUB)
            u_ref[:, cols] = _dot(h_ref[...], w_ref[:, cols])

        def epilogue(sub):
            for hh in range(PROJ_SUB // HEAD_W):
                lo = sub * PROJ_SUB + hh * HEAD_W
                last = CONV_B - 1
                y = cw_ref[last:CONV_B, lo:lo + HEAD_W] * u_ref[off + last:off + last + rows, lo:lo + HEAD_W]
                for tap in range(last):
                    y = y + cw_ref[tap:tap + 1, lo:lo + HEAD_W] * u_ref[off + tap:off + tap + rows, lo:lo + HEAD_W]
                y = _silu(y)
                r = lax.rsqrt(jnp.sum(y * y, axis=-1, keepdims=True) + EPS) * scale
                o_ref[:, lo:lo + HEAD_W] = (y * jnp.where(j == 2, 1.0, r)).astype(BF16)

        n_sub = PROJ_COLS // PROJ_SUB
        project(0)
        for sub in range(n_sub):
            if sub + 1 < n_sub:
                project(sub + 1)
            epilogue(sub)

    @pl.when(j >= 3)
    def _():
        o_ref[...] = _dot(h_ref[PROJ_HALO:, :], w_ref[...]).astype(BF16)


def _gdn_proj(x2, g, w, wsm, conv_w, seq):
    rows = x2.shape[0]
    n_cols = w.shape[1]
    assert seq % PROJ_ROWS == 0 and PROJ_ROWS % PROJ_HALO == 0
    halo_blocks = PROJ_ROWS // PROJ_HALO
    return pl.pallas_call(
        functools.partial(_gdn_proj_kernel, seq=seq),
        grid=(rows // PROJ_ROWS, n_cols // PROJ_COLS),
        in_specs=[
            pl.BlockSpec((PROJ_ROWS, D_MODEL), lambda i, j: (i, 0)),
            pl.BlockSpec((PROJ_HALO, D_MODEL), lambda i, j: (jnp.maximum(i * halo_blocks - 1, 0), 0)),
            pl.BlockSpec((1, D_MODEL), lambda i, j: (0, 0)),
            pl.BlockSpec((D_MODEL, PROJ_COLS), lambda i, j: (0, j)),
            pl.BlockSpec((D_MODEL, LANE), lambda i, j: (0, 0)),
            pl.BlockSpec((CONV_B, PROJ_COLS), lambda i, j: (0, jnp.minimum(j, 2))),
        ],
        out_specs=[
            pl.BlockSpec((PROJ_ROWS, PROJ_COLS), lambda i, j: (i, j)),
            pl.BlockSpec((PROJ_ROWS, LANE), lambda i, j: (i, 0)),
        ],
        out_shape=[
            jax.ShapeDtypeStruct((rows, n_cols), BF16),
            jax.ShapeDtypeStruct((rows, LANE), F32),
        ],
        scratch_shapes=[pltpu.VMEM((PROJ_HALO + PROJ_ROWS, D_MODEL), BF16),
                        pltpu.VMEM((PROJ_HALO + PROJ_ROWS, PROJ_COLS), F32)],
        compiler_params=_params(("parallel", "arbitrary")),
        name="gdn_proj",
    )(x2, x2, g, w, wsm, conv_w)


def _bias_tiles_kernel(tab_ref, diag_ref, corner_ref):
    h = pl.program_id(0)
    t = diag_ref.shape[1]
    cn = corner_ref.shape[1]

    def bias_of(rel):
        b = jnp.full(rel.shape, tab_ref[h, 0], F32)
        for bk in range(1, N_BUCKETS):
            b = jnp.where(rel >= BUCKET_LO[bk - 1], tab_ref[h, bk], b)
        return b

    rel = lax.broadcasted_iota(jnp.int32, (t, t), 0) - lax.broadcasted_iota(jnp.int32, (t, t), 1)
    diag_ref[0] = jnp.where(rel >= 0, bias_of(rel), NEG_INF)
    rel_c = (lax.broadcasted_iota(jnp.int32, (cn, cn), 0) - lax.broadcasted_iota(jnp.int32, (cn, cn), 1)) + cn
    corner_ref[0] = bias_of(rel_c) - tab_ref[h, N_BUCKETS - 1]


def _bias_tiles(rel_bias_t):
    t = ATT_BLOCK
    assert ATT_CORNER + 1 >= BUCKET_LO[-1], "entries outside the corner must sit in the last bucket"
    return pl.pallas_call(
        _bias_tiles_kernel,
        grid=(N_HEADS,),
        in_specs=[pl.BlockSpec(memory_space=pltpu.SMEM)],
        out_specs=[
            pl.BlockSpec((1, t, t), lambda h: (h, 0, 0)),
            pl.BlockSpec((1, ATT_CORNER, ATT_CORNER), lambda h: (h, 0, 0)),
        ],
        out_shape=[
            jax.ShapeDtypeStruct((N_HEADS, t, t), F32),
            jax.ShapeDtypeStruct((N_HEADS, ATT_CORNER, ATT_CORNER), F32),
        ],
        compiler_params=_params(("parallel",)),
        name="bias_tiles",
    )(rel_bias_t)


def _attn_kernel(qi_tab, kj_tab, q_ref, k_ref, v_ref, diag_ref, corner_ref, tab_ref, lam_ref, sg_ref, li_ref,
                 o_ref, qs_ref, va_ref, m_ref, acc_ref):
    hg = pl.program_id(0)
    step = pl.program_id(2)
    qi = qi_tab[step]
    kj = kj_tab[step]
    t = q_ref.shape[1]
    cn = corner_ref.shape[1]
    heads = range(ATT_HEADS)
    cols = [slice(g * HEAD_W, (g + 1) * HEAD_W) for g in heads]

    @pl.when(kj == 0)
    def _():
        lane = lax.broadcasted_iota(jnp.int32, (t, HEAD_W), 1)
        ones_col = jnp.where(lane == 0, 1.0, 0.0).astype(BF16)
        for g in heads:
            q = q_ref[0, :, cols[g]]
            zero = jnp.zeros_like(q)
            qs_ref[g, 0:t, :] = jnp.where(lane < HEAD_DIM_A, q, zero)
            qs_ref[g, t:2 * t, :] = jnp.where(lane >= HEAD_DIM_A, q, zero)
            va_ref[g, :, HEAD_W:2 * HEAD_W] = ones_col
        m_ref[...] = jnp.full(m_ref.shape, NEG_INF, F32)
        acc_ref[...] = jnp.zeros(acc_ref.shape, F32)

    for g in heads:
        va_ref[g, :, 0:HEAD_W] = v_ref[0, :, cols[g]]
    delta = qi - kj

    def sweep(bias_fn, c):
        units = [(g, rc) for g in heads for rc in range(2 * t // ATT_CHUNK)]
        rows = [slice(rc * ATT_CHUNK, (rc + 1) * ATT_CHUNK) for _, rc in units]
        idx = range(len(units))
        s = [bias_fn(_dot_nt(qs_ref[units[i][0], rows[i], :], k_ref[0, :, cols[units[i][0]]]),
                     units[i][0], (units[i][1] * ATT_CHUNK) % t) for i in idx]
        m_prev = [m_ref[units[i][0], rows[i], :] for i in idx]
        m_new = [jnp.maximum(m_prev[i], jnp.broadcast_to(jnp.max(s[i], axis=-1, keepdims=True), (ATT_CHUNK, LANE))
                             + c[units[i][0]]) for i in idx]
        p = [jnp.exp2(s[i] - jnp.concatenate([m_new[i] - c[units[i][0]]] * (t // LANE), axis=1)).astype(BF16)
             for i in idx]
        pv = [_dot(p[i], va_ref[units[i][0]]) for i in idx]
        for i in idx:
            g = units[i][0]
            alpha = jnp.exp2(m_prev[i] - m_new[i])
            acc_ref[g, rows[i], :] = jnp.concatenate([alpha, alpha], axis=1) * acc_ref[g, rows[i], :] + pv[i]
            m_ref[g, rows[i], :] = m_new[i]

    @pl.when(delta >= 1)
    def _():
        near = jnp.where(delta == 1, 1.0, 0.0)

        def bias_fn(s, g, q_row0):
            if q_row0 != 0:
                return s
            top = jnp.concatenate([s[0:cn, 0:t - cn], s[0:cn, t - cn:t] + near * corner_ref[g]], axis=1)
            return jnp.concatenate([top, s[cn:, :]], axis=0)

        sweep(bias_fn, [tab_ref[hg * ATT_HEADS + g, N_BUCKETS - 1] for g in heads])

    @pl.when(delta == 0)
    def _():
        sweep(lambda s, g, q_row0: s + diag_ref[g, q_row0:q_row0 + ATT_CHUNK, :], [0.0] * ATT_HEADS)
        lq = lam_ref[...]
        lam_init = li_ref[0]
        lam = (jnp.exp(jnp.sum(lq[0:1] * lq[1:2], axis=-1, keepdims=True))
               - jnp.exp(jnp.sum(lq[2:3] * lq[3:4], axis=-1, keepdims=True)) + lam_init)
        for g in heads:
            acc = acc_ref[g]
            o0 = acc[0:t, 0:HEAD_W] / acc[0:t, HEAD_W:HEAD_W + 1]
            o1 = acc[t:2 * t, 0:HEAD_W] / acc[t:2 * t, HEAD_W:HEAD_W + 1]
            o = o0 - lam * o1
            o_ref[0, :, cols[g]] = (_rmsnorm(o, sg_ref[...]) * (1.0 - lam_init)).astype(BF16)


def _attention(qkv, bias_diag, bias_corner, rel_bias_t, lam_a, subln, lam_init):
    bsz, seq, _ = qkv.shape
    t = ATT_BLOCK
    nq = seq // t
    assert t % ATT_CHUNK == 0 and ATT_CORNER <= ATT_CHUNK and N_HEADS % ATT_HEADS == 0
    n_groups = N_HEADS // ATT_HEADS
    gw = ATT_HEADS * HEAD_W
    pairs =[(qi, kj) for qi in range(nq) for kj in range(qi + 1)]
    qi_tab = jnp.asarray([p[0] for p in pairs], jnp.int32)
    kj_tab = jnp.asarray([p[1] for p in pairs], jnp.int32)
    grid_spec = pltpu.PrefetchScalarGridSpec(
        num_scalar_prefetch=2,
        grid=(n_groups, bsz, len(pairs)),
        in_specs=[
            pl.BlockSpec((1, t, gw), lambda h, b, s, qt, kt: (b, qt[s], h)),
            pl.BlockSpec((1, t, gw), lambda h, b, s, qt, kt: (b, kt[s], n_groups + h)),
            pl.BlockSpec((1, t, gw), lambda h, b, s, qt, kt: (b, kt[s], 2 * n_groups + h)),
            pl.BlockSpec((ATT_HEADS, t, t), lambda h, b, s, qt, kt: (h, 0, 0)),
            pl.BlockSpec((ATT_HEADS, ATT_CORNER, ATT_CORNER), lambda h, b, s, qt, kt: (h, 0, 0)),
            pl.BlockSpec(memory_space=pltpu.SMEM),
            pl.BlockSpec((4, HEAD_DIM_A), lambda h, b, s, qt, kt: (0, 0)),
            pl.BlockSpec((1, HEAD_W), lambda h, b, s, qt, kt: (0, 0)),
            pl.BlockSpec(memory_space=pltpu.SMEM),
        ],
        out_specs=pl.BlockSpec((1, t, gw), lambda h, b, s, qt, kt: (b, qt[s], h)),
        scratch_shapes=[
            pltpu.VMEM((ATT_HEADS, 2 * t, HEAD_W), BF16),
            pltpu.VMEM((ATT_HEADS, t, 2 * HEAD_W), BF16),
            pltpu.VMEM((ATT_HEADS, 2 * t, LANE), F32),
            pltpu.VMEM((ATT_HEADS, 2 * t, 2 * HEAD_W), F32),
        ],
    )
    return pl.pallas_call(
        _attn_kernel,
        grid_spec=grid_spec,
        out_shape=jax.ShapeDtypeStruct((bsz, seq, D_MODEL), BF16),
        compiler_params=_params(("parallel", "parallel", "arbitrary")),
        name="diff_attention",
    )(qi_tab, kj_tab, qkv, qkv, qkv, bias_diag, bias_corner, rel_bias_t, lam_a, subln, lam_init)


def _split(x, n_pieces):
    pieces = []
    for _ in range(n_pieces):
        piece = x.astype(BF16)
        pieces.append(piece)
        x = x - piece.astype(F32)
    return pieces


def _gdn_kernel(q_ref, k_ref, v_ref, z_ref, sm_ref, alog_ref, dtb_ref, on_ref, tril_ref, lvl_ref,
                o_ref, state_ref):
    c = GDN_CHUNK

    @pl.when(pl.program_id(1) == 0)
    def _():
        state_ref[...] = jnp.zeros(state_ref.shape, F32)

    sm = sm_ref[0]
    beta_all = _sigmoid(sm)
    xa = sm + dtb_ref[...]
    softplus = jnp.maximum(xa, 0.0) + jnp.log1p(jnp.exp(-jnp.abs(xa)))
    g_all = -jnp.exp(alog_ref[...]) * softplus
    gc_all = _dot(tril_ref[...], jnp.concatenate(_split(g_all, 3), axis=0))
    eg_all = jnp.exp(gc_all)
    ed_all = jnp.exp(gc_all[c - 1:c, :] - gc_all)

    ii = lax.broadcasted_iota(jnp.int32, (c, c), 0)
    jj = lax.broadcasted_iota(jnp.int32, (c, c), 1)
    tril = ii >= jj
    strict = ii > jj
    n_levels = lvl_ref.shape[0] - 1

    heads = range(N_HEADS)
    cols = [slice(h * HEAD_W, (h + 1) * HEAD_W) for h in heads]
    beta = [beta_all[:, h:h + 1] for h in heads]
    gcr = [jnp.broadcast_to(gc_all[:, N_HEADS + h:N_HEADS + h + 1], (c, c)) for h in heads]
    eg = [eg_all[:, N_HEADS + h:N_HEADS + h + 1] for h in heads]
    ed = [ed_all[:, N_HEADS + h:N_HEADS + h + 1] for h in heads]
    k = [k_ref[0, :, cols[h]] for h in heads]
    kq = [_dot_nt(jnp.concatenate([k[h], q_ref[0, :, cols[h]]], axis=0), k[h]) for h in heads]
    decay = [jnp.where(tril, jnp.exp(gcr[h] - gcr[h].T), 0.0) for h in heads]
    a_b = [jnp.where(strict, kq[h][0:c] * (decay[h] * beta[h]), 0.0).astype(BF16) for h in heads]
    attn_b = [(kq[h][c:2 * c] * decay[h]).astype(BF16) for h in heads]
    inv = [lvl_ref[0] - a_b[h] * lvl_ref[1] for h in heads]
    for lv in range(2, n_levels + 1):
        x = [_dot(a_b[h] * lvl_ref[lv], inv[h]).astype(BF16) for h in heads]
        inv = [inv[h] - _dot(inv[h], x[h]).astype(BF16) for h in heads]
    rhs = [jnp.concatenate([k[h] * (beta[h] * eg[h]).astype(BF16),
                            v_ref[0, :, cols[h]] * beta[h].astype(BF16)], axis=1) for h in heads]
    wu = [_dot(inv[h], rhs[h]).astype(BF16) for h in heads]
    aw = [_dot(attn_b[h], wu[h]) for h in heads]
    pn = [_dot_tn(k[h] * ed[h].astype(BF16), wu[h]) for h in heads]
    q_eff = [q_ref[0, :, cols[h]].astype(F32) * eg[h] - aw[h][:, 0:HEAD_W] for h in heads]
    state = [state_ref[h] for h in heads]
    r = [_dot(jnp.concatenate([pn[h][:, 0:HEAD_W], q_eff[h]], axis=0).astype(BF16), state[h].astype(BF16))
         for h in heads]
    for h in heads:
        state_ref[h] = state[h] * eg[h][c - 1:c, :] + pn[h][:, HEAD_W:2 * HEAD_W] - r[h][0:DK_B]
    for h in heads:
        o = r[h][DK_B:DK_B + c] + aw[h][:, HEAD_W:2 * HEAD_W]
        z = z_ref[0, :, cols[h]].astype(F32)
        o_ref[0, :, cols[h]] = (_rmsnorm(o, on_ref[...]) * _silu(z)).astype(BF16)


def _gdn_constants():
    c = GDN_CHUNK
    r = np.arange(c)
    tril = (r[None, :] <= r[:, None]).astype(np.float32)
    tril3 = np.concatenate([tril, tril, tril], axis=1)
    levels = [np.eye(c, dtype=np.float32)]
    s = 1
    while s < c:
        same = (r[:, None] ^ r[None, :]) < 2 * s
        levels.append((same & ((r[:, None] & s) != 0) & ((r[None, :] & s) == 0)).astype(np.float32))
        s *= 2
    return jnp.asarray(tril3, BF16), jnp.asarray(np.stack(levels), BF16)


def _gdn(proj_b, small, alog_row, dtb_row, onorm):
    bsz, seq, _ = proj_b.shape
    c = GDN_CHUNK
    assert c == LANE and seq % c == 0
    tril3, levels = _gdn_constants()
    blk = lambda col: pl.BlockSpec((1, c, D_MODEL), lambda b, s: (b, s, col))
    return pl.pallas_call(
        _gdn_kernel,
        grid=(bsz, seq // c),
        in_specs=[
            blk(0), blk(1), blk(2), blk(3),
            pl.BlockSpec((1, c, LANE), lambda b, s: (b, s, 0)),
            pl.BlockSpec((1, LANE), lambda b, s: (0, 0)),
            pl.BlockSpec((1, LANE), lambda b, s: (0, 0)),
            pl.BlockSpec((1, HEAD_W), lambda b, s: (0, 0)),
            _resident(tril3.shape), _resident(levels.shape),
        ],
        out_specs=pl.BlockSpec((1, c, D_MODEL), lambda b, s: (b, s, 0)),
        out_shape=jax.ShapeDtypeStruct((bsz, seq, D_MODEL), BF16),
        scratch_shapes=[pltpu.VMEM((N_HEADS, DK_B, HEAD_W), F32)],
        compiler_params=_params(("parallel", "arbitrary")),
        name="gated_delta_rule",
    )(proj_b, proj_b, proj_b, proj_b, small, alog_row, dtb_row, onorm, tril3, levels)


def _merge_kernel(x_ref, oa_ref, ob_ref, ga_ref, gb_ref, wa_ref, wb_ref, wo_ref, o_ref):
    ya = _dot(oa_ref[...], wa_ref[...])
    yb = _dot(ob_ref[...], wb_ref[...])
    mixed = _sigmoid(ga_ref[...].astype(F32)) * ya + _sigmoid(gb_ref[...].astype(F32)) * yb
    o_ref[...] = x_ref[...] + _dot(mixed.astype(BF16), wo_ref[...])


def _merge(x2, oa, ob, proj_b, wa, wb, wo):
    rows = x2.shape[0]
    gate_block = 4
    row_spec = pl.BlockSpec((MERGE_ROWS, D_MODEL), lambda i: (i, 0))
    return pl.pallas_call(
        _merge_kernel,
        grid=(rows // MERGE_ROWS,),
        in_specs=[
            row_spec, row_spec, row_spec,
            pl.BlockSpec((MERGE_ROWS, D_MODEL), lambda i: (i, gate_block)),
            pl.BlockSpec((MERGE_ROWS, D_MODEL), lambda i: (i, gate_block + 1)),
            _resident((D_MODEL, D_MODEL)), _resident((D_MODEL, D_MODEL)), _resident((D_MODEL, D_MODEL)),
        ],
        out_specs=row_spec,
        out_shape=jax.ShapeDtypeStruct((rows, D_MODEL), F32),
        compiler_params=_params(("parallel",)),
        name="branch_merge",
    )(x2, oa, ob, proj_b, proj_b, wa, wb, wo)


def _ffn_ple_kernel(x_ref, halo_ref, p_ref, fg_ref, wup_ref, cw_ref, cb_ref, wdn_ref, pg_ref, wg_ref, wp_ref,
                    o_ref, act_ref, *, seq):
    rows = x_ref.shape[0]
    x = x_ref[...]
    fg = fg_ref[...]
    keep = jnp.where((pl.program_id(0) * rows) % seq == 0, 0.0, 1.0)
    h_ext = jnp.concatenate([_rmsnorm(halo_ref[...], fg) * keep, _rmsnorm(x, fg)], axis=0).astype(BF16)
    off = HALO - (CONV_FF - 1)

    def conv(u, lo):
        y = cb_ref[:, lo:lo + FFN_COLS]
        for tap in range(CONV_FF):
            y = y + cw_ref[tap:tap + 1, lo:lo + FFN_COLS] * u[off + tap:off + tap + rows]
        return y

    for c in range(D_FF // FFN_COLS):
        lo = c * FFN_COLS
        yg = conv(_dot(h_ext, wup_ref[:, lo:lo + FFN_COLS]), lo)
        yv = conv(_dot(h_ext, wup_ref[:, D_FF + lo:D_FF + lo + FFN_COLS]), D_FF + lo)
        act_ref[:, lo:lo + FFN_COLS] = (_silu(yg) * yv).astype(BF16)

    x1 = x + _dot(act_ref[...], wdn_ref[...])
    h2 = _rmsnorm(x1, pg_ref[...]).astype(BF16)
    gate = _sigmoid(_dot(h2, wg_ref[...]))
    o_ref[...] = x1 + gate * _dot(p_ref[...].astype(BF16), wp_ref[...])


def _ffn_ple(x2, p2, fg, wup, cw, cb, wdn, pg, wg, wp, seq):
    rows = x2.shape[0]
    assert seq % FFN_ROWS == 0 and D_FF % FFN_COLS == 0
    halo_blocks = FFN_ROWS // HALO
    return pl.pallas_call(
        functools.partial(_ffn_ple_kernel, seq=seq),
        grid=(rows // FFN_ROWS,),
        in_specs=[
            pl.BlockSpec((FFN_ROWS, D_MODEL), lambda i: (i, 0)),
            pl.BlockSpec((HALO, D_MODEL), lambda i: (jnp.maximum(i * halo_blocks - 1, 0), 0)),
            pl.BlockSpec((FFN_ROWS, D_PLE), lambda i: (i, 0)),
            _resident((1, D_MODEL)),
            _resident((D_MODEL, 2 * D_FF)),
            _resident((CONV_FF, 2 * D_FF)),
            _resident((1, 2 * D_FF)),
            _resident((D_FF, D_MODEL)),
            _resident((1, D_MODEL)),
            _resident((D_MODEL, D_MODEL)),
            _resident((D_PLE, D_MODEL)),
        ],
        out_specs=pl.BlockSpec((FFN_ROWS, D_MODEL), lambda i: (i, 0)),
        out_shape=jax.ShapeDtypeStruct((rows, D_MODEL), F32),
        scratch_shapes=[pltpu.VMEM((FFN_ROWS, D_FF), BF16)],
        compiler_params=_params(("arbitrary",)),
        name="conv_ffn_ple",
    )(x2, x2, p2, fg, wup, cw, cb, wdn, pg, wg, wp)


def kernel(x, p, w_in, attn_norm, qnorm_a, knorm_a, lam_a, subln_a, rel_bias, conv_b, a_log_b, dt_bias_b, onorm_b, w_br_a, w_br_b, w_out, ffn_norm, w_up, conv_ff, conv_ff_bias, w_down, ple_norm, w_ple_gate, w_ple_proj):
    bsz, seq, _ = x.shape
    depth = w_in.shape[0]
    rows = bsz * seq
    assert rows % PROJ_ROWS == 0 and seq % ATT_BLOCK == 0 and rows % MERGE_ROWS == 0

    n_a = 3 * D_MODEL
    n_b = 4 * D_MODEL
    w_in_b = w_in.astype(BF16)
    rel_bias_t = rel_bias.astype(F32).T * LOG2E
    bias_diag, bias_corner = _bias_tiles(rel_bias_t)
    n_maps = D_MODEL // HEAD_DIM_A

    x2 = x.reshape(rows, D_MODEL).astype(F32)
    for i in range(depth):
        lam_init = 0.8 - 0.6 * math.exp(-0.3 * i)
        w_a = w_in_b[i, :, 0:n_a]
        w_b = jnp.concatenate([w_in_b[i, :, n_a:n_a + n_b], w_in_b[i, :, n_a + n_b + 2 * N_HEADS:]], axis=1)
        w_small = jnp.pad(w_in_b[i, :, n_a + n_b:n_a + n_b + 2 * N_HEADS], ((0, 0), (0, LANE - 2 * N_HEADS)))
        gains = jnp.stack([
            jnp.tile(qnorm_a[i].astype(F32), n_maps) * (HEAD_DIM_A ** -0.5 * LOG2E),
            jnp.tile(knorm_a[i].astype(F32), n_maps),
            jnp.ones((D_MODEL,), F32),
        ]).reshape(3, 1, D_MODEL)
        g_attn = attn_norm[i].astype(F32).reshape(1, D_MODEL)

        qkv = _attn_proj(x2, g_attn, w_a, gains)
        proj_b, small = _gdn_proj(x2, g_attn, w_b, w_small, conv_b[i].astype(F32), seq)

        oa = _attention(qkv.reshape(bsz, seq, n_a), bias_diag, bias_corner, rel_bias_t, lam_a[i].astype(F32),
                        subln_a[i].astype(F32).reshape(1, HEAD_W), jnp.full((1,), lam_init, F32))
        head_lanes = (N_HEADS, LANE - 2 * N_HEADS)
        ob = _gdn(proj_b.reshape(bsz, seq, -1), small.reshape(bsz, seq, LANE),
                  jnp.pad(a_log_b[i].astype(F32), head_lanes).reshape(1, LANE),
                  jnp.pad(dt_bias_b[i].astype(F32), head_lanes).reshape(1, LANE),
                  onorm_b[i].astype(F32).reshape(1, HEAD_W))

        x2 = _merge(x2, oa.reshape(rows, D_MODEL), ob.reshape(rows, D_MODEL), proj_b,
                    w_br_a[i].astype(BF16), w_br_b[i].astype(BF16), w_out[i].astype(BF16))

        x2 = _ffn_ple(x2, p[i].reshape(rows, D_PLE), ffn_norm[i].astype(F32).reshape(1, D_MODEL),
                      w_up[i].astype(BF16), conv_ff[i].astype(F32), conv_ff_bias[i].astype(F32).reshape(1, 2 * D_FF),
                      w_down[i].astype(BF16), ple_norm[i].astype(F32).reshape(1, D_MODEL),
                      w_ple_gate[i].astype(BF16), w_ple_proj[i].astype(BF16), seq)
    return x2.reshape(bsz, seq, D_MODEL).astype(x.dtype)
```

```python
import functools
import math

import numpy as np
import jax
import jax.numpy as jnp
from jax import lax
from jax.experimental import pallas as pl
from jax.experimental.pallas import tpu as pltpu

F32 = jnp.float32
BF16 = jnp.bfloat16

D_MODEL = 1024
N_HEADS = 8
HEAD_W = 128
HEAD_DIM_A = 64
DK_B = 128
CONV_B = 4
CHUNK_B = 64
D_FF = 2816
CONV_FF = 3
D_PLE = 256
N_BUCKETS = 32
MAX_DISTANCE = 128
EPS = 1e-6
NEG_INF = -1e30

LANE = 128
SUBLANE = 8
VMEM_LIMIT_BYTES = 56 * 1024 * 1024

PROJ_ROWS = 1024
PROJ_COLS = 1024
PROJ_SUB = 256
ATT_BLOCK = 512
ATT_CHUNK = 512
ATT_HEADS = 4
ATT_ISSUE_HEADS = 2
ATT_CORNER = 128
LOG2E = math.log2(math.e)
GDN_CHUNK = 128
MERGE_ROWS = 512
FFN_ROWS = 512
FFN_COLS = 256
HALO = SUBLANE
PROJ_HALO = 2 * SUBLANE


def _bucket_thresholds():
    n = np.arange(0, 4 * MAX_DISTANCE)
    max_exact = N_BUCKETS // 2
    nf = np.maximum(n, 1).astype(np.float32)
    large = max_exact + (np.log(nf / np.float32(max_exact)) / np.float32(math.log(MAX_DISTANCE / max_exact))
                         * np.float32(N_BUCKETS - max_exact)).astype(np.int32)
    bucket = np.where(n < max_exact, n, np.minimum(large, N_BUCKETS - 1))
    return [int(np.argmax(bucket >= b)) for b in range(1, N_BUCKETS)]


BUCKET_LO = _bucket_thresholds()


def _params(semantics):
    return pltpu.CompilerParams(dimension_semantics=semantics, vmem_limit_bytes=VMEM_LIMIT_BYTES)


def _resident(shape):
    zeros = (0,) * len(shape)
    return pl.BlockSpec(shape, lambda *_: zeros, pipeline_mode=pl.Buffered(1))


def _rmsnorm(x, g):
    ms = jnp.mean(x * x, axis=-1, keepdims=True)
    return x * lax.rsqrt(ms + EPS) * g


def _sigmoid(x):
    return 1.0 / (1.0 + jnp.exp(-x))


def _silu(x):
    return x * _sigmoid(x)


def _dot(a, b):
    return jnp.dot(a, b, preferred_element_type=F32)


def _dot_nt(a, b):
    return lax.dot_general(a, b, (((1,), (1,)), ((), ())), preferred_element_type=F32)


def _dot_tn(a, b):
    return lax.dot_general(a, b, (((0,), (0,)), ((), ())), preferred_element_type=F32)


def _attn_proj_kernel(x_ref, g_ref, w_ref, gain_ref, gsum_ref, gexp_ref, o_ref, h_ref):
    j = pl.program_id(1)

    @pl.when(j == 0)
    def _():
        h_ref[...] = _rmsnorm(x_ref[...], g_ref[...]).astype(BF16)

    @pl.when(j < 2)
    def _():
        n_sub = PROJ_COLS // PROJ_SUB

        def project(sub):
            return _dot(h_ref[...], w_ref[:, sub * PROJ_SUB:(sub + 1) * PROJ_SUB])

        def normalise(sub, y):
            cols = slice(sub * PROJ_SUB, (sub + 1) * PROJ_SUB)
            ss = _dot((y * y).astype(BF16), gsum_ref[cols, :])
            r = lax.rsqrt(ss * (1.0 / HEAD_DIM_A) + EPS)
            r_hi = r.astype(BF16)
            r_lo = (r - r_hi.astype(F32)).astype(BF16)
            rb = _dot(jnp.concatenate([r_hi, r_lo], axis=1), gexp_ref[:, cols])
            o_ref[:, cols] = (y * rb * gain_ref[:, cols]).astype(BF16)

        y = project(0)
        for sub in range(n_sub):
            y_next = project(sub + 1) if sub + 1 < n_sub else None
            normalise(sub, y)
            y = y_next

    @pl.when(j == 2)
    def _():
        o_ref[...] = _dot(h_ref[...], w_ref[...]).astype(BF16)


def _attn_proj(x2, g, w, gains):
    rows = x2.shape[0]
    n_groups = D_MODEL // HEAD_DIM_A
    col = np.arange(D_MODEL) // HEAD_DIM_A
    gsum = (col[:, None] == np.arange(LANE)[None, :]).astype(np.float32)
    gexp = np.concatenate([gsum.T, gsum.T], axis=0)
    assert n_groups <= LANE
    return pl.pallas_call(
        _attn_proj_kernel,
        grid=(rows // PROJ_ROWS, 3),
        in_specs=[
            pl.BlockSpec((PROJ_ROWS, D_MODEL), lambda i, j: (i, 0)),
            pl.BlockSpec((1, D_MODEL), lambda i, j: (0, 0)),
            pl.BlockSpec((D_MODEL, PROJ_COLS), lambda i, j: (0, j)),
            pl.BlockSpec((None, 1, PROJ_COLS), lambda i, j: (j, 0, 0)),
            pl.BlockSpec((D_MODEL, LANE), lambda i, j: (0, 0)),
            pl.BlockSpec((2 * LANE, D_MODEL), lambda i, j: (0, 0)),
        ],
        out_specs=pl.BlockSpec((PROJ_ROWS, PROJ_COLS), lambda i, j: (i, j)),
        out_shape=jax.ShapeDtypeStruct((rows, 3 * D_MODEL), BF16),
        scratch_shapes=[pltpu.VMEM((PROJ_ROWS, D_MODEL), BF16)],
        compiler_params=_params(("parallel", "arbitrary")),
        name="attn_proj",
    )(x2, g, w, gains, jnp.asarray(gsum, BF16), jnp.asarray(gexp, BF16))


def _gdn_proj_kernel(x_ref, halo_ref, g_ref, w_ref, wsm_ref, cw_ref, o_ref, osm_ref, h_ref, u_ref, *, seq):
    i = pl.program_id(0)
    j = pl.program_id(1)
    rows = x_ref.shape[0]

    @pl.when(j == 0)
    def _():
        g = g_ref[...]
        keep = jnp.where((i * rows) % seq == 0, 0.0, 1.0)
        h_ref[0:PROJ_HALO, :] = (_rmsnorm(halo_ref[...], g) * keep).astype(BF16)
        h = _rmsnorm(x_ref[...], g).astype(BF16)
        h_ref[PROJ_HALO:, :] = h
        osm_ref[...] = _dot(h, wsm_ref[...])

    @pl.when(j < 3)
    def _():
        off = PROJ_HALO - (CONV_B - 1)
        scale = jnp.where(j == 0, DK_B ** -0.5, 1.0)

        def project(sub):
            cols = slice(sub * PROJ_SUB, (sub + 1) * PROJ_SUB)
            u_ref[:, cols] = _dot(h_ref[...], w_ref[:, cols])

        def epilogue(sub):
            for hh in range(PROJ_SUB // HEAD_W):
                lo = sub * PROJ_SUB + hh * HEAD_W
                last = CONV_B - 1
                y = cw_ref[last:CONV_B, lo:lo + HEAD_W] * u_ref[off + last:off + last + rows, lo:lo + HEAD_W]
                for tap in range(last):
                    y = y + cw_ref[tap:tap + 1, lo:lo + HEAD_W] * u_ref[off + tap:off + tap + rows, lo:lo + HEAD_W]
                y = _silu(y)
                r = lax.rsqrt(jnp.sum(y * y, axis=-1, keepdims=True) + EPS) * scale
                o_ref[:, lo:lo + HEAD_W] = (y * jnp.where(j == 2, 1.0, r)).astype(BF16)

        n_sub = PROJ_COLS // PROJ_SUB
        project(0)
        for sub in range(n_sub):
            if sub + 1 < n_sub:
                project(sub + 1)
            epilogue(sub)

    @pl.when(j >= 3)
    def _():
        o_ref[...] = _dot(h_ref[PROJ_HALO:, :], w_ref[...]).astype(BF16)


def _gdn_proj(x2, g, w, wsm, conv_w, seq):
    rows = x2.shape[0]
    n_cols = w.shape[1]
    assert seq % PROJ_ROWS == 0 and PROJ_ROWS % PROJ_HALO == 0
    halo_blocks = PROJ_ROWS // PROJ_HALO
    return pl.pallas_call(
        functools.partial(_gdn_proj_kernel, seq=seq),
        grid=(rows // PROJ_ROWS, n_cols // PROJ_COLS),
        in_specs=[
            pl.BlockSpec((PROJ_ROWS, D_MODEL), lambda i, j: (i, 0)),
            pl.BlockSpec((PROJ_HALO, D_MODEL), lambda i, j: (jnp.maximum(i * halo_blocks - 1, 0), 0)),
            pl.BlockSpec((1, D_MODEL), lambda i, j: (0, 0)),
            pl.BlockSpec((D_MODEL, PROJ_COLS), lambda i, j: (0, j)),
            pl.BlockSpec((D_MODEL, LANE), lambda i, j: (0, 0)),
            pl.BlockSpec((CONV_B, PROJ_COLS), lambda i, j: (0, jnp.minimum(j, 2))),
        ],
        out_specs=[
            pl.BlockSpec((PROJ_ROWS, PROJ_COLS), lambda i, j: (i, j)),
            pl.BlockSpec((PROJ_ROWS, LANE), lambda i, j: (i, 0)),
        ],
        out_shape=[
            jax.ShapeDtypeStruct((rows, n_cols), BF16),
            jax.ShapeDtypeStruct((rows, LANE), F32),
        ],
        scratch_shapes=[pltpu.VMEM((PROJ_HALO + PROJ_ROWS, D_MODEL), BF16),
                        pltpu.VMEM((PROJ_HALO + PROJ_ROWS, PROJ_COLS), F32)],
        compiler_params=_params(("parallel", "arbitrary")),
        name="gdn_proj",
    )(x2, x2, g, w, wsm, conv_w)


def _bias_tiles_kernel(tab_ref, diag_ref, corner_ref):
    h = pl.program_id(0)
    t = diag_ref.shape[1]
    cn = corner_ref.shape[1]

    def bias_of(rel):
        b = jnp.full(rel.shape, tab_ref[h, 0], F32)
        for bk in range(1, N_BUCKETS):
            b = jnp.where(rel >= BUCKET_LO[bk - 1], tab_ref[h, bk], b)
        return b

    rel = lax.broadcasted_iota(jnp.int32, (t, t), 0) - lax.broadcasted_iota(jnp.int32, (t, t), 1)
    diag_ref[0] = jnp.where(rel >= 0, bias_of(rel), NEG_INF)
    rel_c = (lax.broadcasted_iota(jnp.int32, (cn, cn), 0) - lax.broadcasted_iota(jnp.int32, (cn, cn), 1)) + cn
    corner_ref[0] = bias_of(rel_c) - tab_ref[h, N_BUCKETS - 1]


def _bias_tiles(rel_bias_t):
    t = ATT_BLOCK
    assert ATT_CORNER + 1 >= BUCKET_LO[-1], "entries outside the corner must sit in the last bucket"
    return pl.pallas_call(
        _bias_tiles_kernel,
        grid=(N_HEADS,),
        in_specs=[pl.BlockSpec(memory_space=pltpu.SMEM)],
        out_specs=[
            pl.BlockSpec((1, t, t), lambda h: (h, 0, 0)),
            pl.BlockSpec((1, ATT_CORNER, ATT_CORNER), lambda h: (h, 0, 0)),
        ],
        out_shape=[
            jax.ShapeDtypeStruct((N_HEADS, t, t), F32),
            jax.ShapeDtypeStruct((N_HEADS, ATT_CORNER, ATT_CORNER), F32),
        ],
        compiler_params=_params(("parallel",)),
        name="bias_tiles",
    )(rel_bias_t)


def _attn_kernel(qi_tab, kj_tab, q_ref, k_ref, v_ref, diag_ref, corner_ref, tab_ref, lam_ref, sg_ref, li_ref,
                 o_ref, qs_ref, va_ref, m_ref, acc_ref):
    hg = pl.program_id(0)
    step = pl.program_id(2)
    qi = qi_tab[step]
    kj = kj_tab[step]
    t = q_ref.shape[1]
    cn = corner_ref.shape[1]
    heads = range(ATT_HEADS)
    cols = [slice(g * HEAD_W, (g + 1) * HEAD_W) for g in heads]

    @pl.when(kj == 0)
    def _():
        lane = lax.broadcasted_iota(jnp.int32, (t, HEAD_W), 1)
        ones_col = jnp.ones((t, HEAD_W), BF16)
        for g in heads:
            q = q_ref[0, :, cols[g]]
            zero = jnp.zeros_like(q)
            qs_ref[g, 0:t, :] = jnp.where(lane < HEAD_DIM_A, q, zero)
            qs_ref[g, t:2 * t, :] = jnp.where(lane >= HEAD_DIM_A, q, zero)
            va_ref[g, :, HEAD_W:2 * HEAD_W] = ones_col
        m_ref[...] = jnp.full(m_ref.shape, NEG_INF, F32)
        acc_ref[...] = jnp.zeros(acc_ref.shape, F32)

    for g in heads:
        va_ref[g, :, 0:HEAD_W] = v_ref[0, :, cols[g]]
    delta = qi - kj

    def sweep(bias_fn, c, kv_len):
        for g0 in range(0, ATT_HEADS, ATT_ISSUE_HEADS):
            units = [(g, rc) for g in range(g0, g0 + ATT_ISSUE_HEADS) for rc in range(2 * t // ATT_CHUNK)]
            idx = range(len(units))
            rows = [slice(rc * ATT_CHUNK, (rc + 1) * ATT_CHUNK) for _, rc in units]
            row0 = [(rc * ATT_CHUNK) % t for _, rc in units]
            kv = [kv_len(r0) for r0 in row0]
            hd = [g for g, _ in units]
            s = [bias_fn(_dot_nt(qs_ref[hd[i], rows[i], :], k_ref[0, 0:kv[i], cols[hd[i]]]), hd[i], row0[i])
                 for i in idx]
            m_prev = [m_ref[hd[i], rows[i], :] for i in idx]
            m_new = [jnp.maximum(m_prev[i], jnp.broadcast_to(jnp.max(s[i], axis=-1, keepdims=True), (ATT_CHUNK, LANE))
                                 + c[hd[i]]) for i in idx]
            p = [jnp.exp2(s[i] - jnp.concatenate([m_new[i] - c[hd[i]]] * (kv[i] // LANE), axis=1)).astype(BF16)
                 for i in idx]
            pv = [_dot(p[i], va_ref[hd[i], 0:kv[i], :]) for i in idx]
            for i in idx:
                alpha = jnp.exp2(m_prev[i] - m_new[i])
                acc_ref[hd[i], rows[i], :] = jnp.concatenate([alpha, alpha], axis=1) * acc_ref[hd[i], rows[i], :] + pv[i]
                m_ref[hd[i], rows[i], :] = m_new[i]

    @pl.when(delta >= 1)
    def _():
        near = jnp.where(delta == 1, 1.0, 0.0)

        def bias_fn(s, g, q_row0):
            if q_row0 != 0:
                return s
            top = jnp.concatenate([s[0:cn, 0:t - cn], s[0:cn, t - cn:t] + near * corner_ref[g]], axis=1)
            return jnp.concatenate([top, s[cn:, :]], axis=0)

        sweep(bias_fn, [tab_ref[hg * ATT_HEADS + g, N_BUCKETS - 1] for g in heads], lambda q_row0: t)

    @pl.when(delta == 0)
    def _():
        sweep(lambda s, g, q_row0: s + diag_ref[g, q_row0:q_row0 + ATT_CHUNK, 0:q_row0 + ATT_CHUNK],
              [0.0] * ATT_HEADS, lambda q_row0: q_row0 + ATT_CHUNK)
        lq = lam_ref[...]
        lam_init = li_ref[0]
        lam = (jnp.exp(jnp.sum(lq[0:1] * lq[1:2], axis=-1, keepdims=True))
               - jnp.exp(jnp.sum(lq[2:3] * lq[3:4], axis=-1, keepdims=True)) + lam_init)
        for g in heads:
            acc = acc_ref[g]
            o0 = acc[0:t, 0:HEAD_W] / acc[0:t, HEAD_W:2 * HEAD_W]
            o1 = acc[t:2 * t, 0:HEAD_W] / acc[t:2 * t, HEAD_W:2 * HEAD_W]
            o = o0 - lam * o1
            o_ref[0, :, cols[g]] = (_rmsnorm(o, sg_ref[...]) * (1.0 - lam_init)).astype(BF16)


def _attention(qkv, bias_diag, bias_corner, rel_bias_t, lam_a, subln, lam_init):
    bsz, seq, _ = qkv.shape
    t = ATT_BLOCK
    nq = seq // t
    assert t % ATT_CHUNK == 0 and ATT_CORNER <= ATT_CHUNK and N_HEADS % ATT_HEADS == 0
    n_groups = N_HEADS // ATT_HEADS
    gw = ATT_HEADS * HEAD_W
    pairs =[(qi, kj) for qi in range(nq) for kj in range(qi + 1)]
    qi_tab = jnp.asarray([p[0] for p in pairs], jnp.int32)
    kj_tab = jnp.asarray([p[1] for p in pairs], jnp.int32)
    grid_spec = pltpu.PrefetchScalarGridSpec(
        num_scalar_prefetch=2,
        grid=(n_groups, bsz, len(pairs)),
        in_specs=[
            pl.BlockSpec((1, t, gw), lambda h, b, s, qt, kt: (b, qt[s], h)),
            pl.BlockSpec((1, t, gw), lambda h, b, s, qt, kt: (b, kt[s], n_groups + h)),
            pl.BlockSpec((1, t, gw), lambda h, b, s, qt, kt: (b, kt[s], 2 * n_groups + h)),
            pl.BlockSpec((ATT_HEADS, t, t), lambda h, b, s, qt, kt: (h, 0, 0)),
            pl.BlockSpec((ATT_HEADS, ATT_CORNER, ATT_CORNER), lambda h, b, s, qt, kt: (h, 0, 0)),
            pl.BlockSpec(memory_space=pltpu.SMEM),
            pl.BlockSpec((4, HEAD_DIM_A), lambda h, b, s, qt, kt: (0, 0)),
            pl.BlockSpec((1, HEAD_W), lambda h, b, s, qt, kt: (0, 0)),
            pl.BlockSpec(memory_space=pltpu.SMEM),
        ],
        out_specs=pl.BlockSpec((1, t, gw), lambda h, b, s, qt, kt: (b, qt[s], h)),
        scratch_shapes=[
            pltpu.VMEM((ATT_HEADS, 2 * t, HEAD_W), BF16),
            pltpu.VMEM((ATT_HEADS, t, 2 * HEAD_W), BF16),
            pltpu.VMEM((ATT_HEADS, 2 * t, LANE), F32),
            pltpu.VMEM((ATT_HEADS, 2 * t, 2 * HEAD_W), F32),
        ],
    )
    return pl.pallas_call(
        _attn_kernel,
        grid_spec=grid_spec,
        out_shape=jax.ShapeDtypeStruct((bsz, seq, D_MODEL), BF16),
        compiler_params=_params(("parallel", "parallel", "arbitrary")),
        name="diff_attention",
    )(qi_tab, kj_tab, qkv, qkv, qkv, bias_diag, bias_corner, rel_bias_t, lam_a, subln, lam_init)


def _split(x, n_pieces):
    pieces = []
    for _ in range(n_pieces):
        piece = x.astype(BF16)
        pieces.append(piece)
        x = x - piece.astype(F32)
    return pieces


def _gdn_kernel(q_ref, k_ref, v_ref, z_ref, sm_ref, alog_ref, dtb_ref, on_ref, tril_ref, lvl_ref,
                o_ref, state_ref):
    c = GDN_CHUNK

    @pl.when(pl.program_id(1) == 0)
    def _():
        state_ref[...] = jnp.zeros(state_ref.shape, F32)

    sm = sm_ref[0]
    beta_all = _sigmoid(sm)
    xa = sm + dtb_ref[...]
    softplus = jnp.maximum(xa, 0.0) + jnp.log1p(jnp.exp(-jnp.abs(xa)))
    g_all = -jnp.exp(alog_ref[...]) * softplus
    gc_all = _dot(tril_ref[...], jnp.concatenate(_split(g_all, 3), axis=0))
    eg_all = jnp.exp(gc_all)
    ed_all = jnp.exp(gc_all[c - 1:c, :] - gc_all)

    ii = lax.broadcasted_iota(jnp.int32, (c, c), 0)
    jj = lax.broadcasted_iota(jnp.int32, (c, c), 1)
    tril = ii >= jj
    strict = ii > jj
    n_levels = lvl_ref.shape[0] - 1

    heads = range(N_HEADS)
    cols = [slice(h * HEAD_W, (h + 1) * HEAD_W) for h in heads]
    beta = [beta_all[:, h:h + 1] for h in heads]
    gcr = [jnp.broadcast_to(gc_all[:, N_HEADS + h:N_HEADS + h + 1], (c, c)) for h in heads]
    eg = [eg_all[:, N_HEADS + h:N_HEADS + h + 1] for h in heads]
    ed = [ed_all[:, N_HEADS + h:N_HEADS + h + 1] for h in heads]
    k = [k_ref[0, :, cols[h]] for h in heads]
    kq = [_dot_nt(jnp.concatenate([k[h], q_ref[0, :, cols[h]]], axis=0), k[h]) for h in heads]
    decay = [jnp.where(tril, jnp.exp(gcr[h] - gcr[h].T), 0.0) for h in heads]
    a_b = [jnp.where(strict, kq[h][0:c] * (decay[h] * beta[h]), 0.0).astype(BF16) for h in heads]
    attn_b = [(kq[h][c:2 * c] * decay[h]).astype(BF16) for h in heads]
    inv = [lvl_ref[0] - a_b[h] * lvl_ref[1] for h in heads]
    for lv in range(2, n_levels + 1):
        x = [_dot(a_b[h] * lvl_ref[lv], inv[h]).astype(BF16) for h in heads]
        inv = [inv[h] - _dot(inv[h], x[h]).astype(BF16) for h in heads]
    rhs = [jnp.concatenate([k[h] * (beta[h] * eg[h]).astype(BF16),
                            v_ref[0, :, cols[h]] * beta[h].astype(BF16)], axis=1) for h in heads]
    wu = [_dot(inv[h], rhs[h]).astype(BF16) for h in heads]
    aw = [_dot(attn_b[h], wu[h]) for h in heads]
    pn = [_dot_tn(k[h] * ed[h].astype(BF16), wu[h]) for h in heads]
    q_eff = [q_ref[0, :, cols[h]].astype(F32) * eg[h] - aw[h][:, 0:HEAD_W] for h in heads]
    state = [state_ref[h] for h in heads]
    r = [_dot(jnp.concatenate([pn[h][:, 0:HEAD_W], q_eff[h]], axis=0).astype(BF16), state[h].astype(BF16))
         for h in heads]
    for h in heads:
        state_ref[h] = state[h] * eg[h][c - 1:c, :] + pn[h][:, HEAD_W:2 * HEAD_W] - r[h][0:DK_B]
    for h in heads:
        o = r[h][DK_B:DK_B + c] + aw[h][:, HEAD_W:2 * HEAD_W]
        z = z_ref[0, :, cols[h]].astype(F32)
        o_ref[0, :, cols[h]] = (_rmsnorm(o, on_ref[...]) * _silu(z)).astype(BF16)


def _gdn_constants():
    c = GDN_CHUNK
    r = np.arange(c)
    tril = (r[None, :] <= r[:, None]).astype(np.float32)
    tril3 = np.concatenate([tril, tril, tril], axis=1)
    levels = [np.eye(c, dtype=np.float32)]
    s = 1
    while s < c:
        same = (r[:, None] ^ r[None, :]) < 2 * s
        levels.append((same & ((r[:, None] & s) != 0) & ((r[None, :] & s) == 0)).astype(np.float32))
        s *= 2
    return jnp.asarray(tril3, BF16), jnp.asarray(np.stack(levels), BF16)


def _gdn(proj_b, small, alog_row, dtb_row, onorm):
    bsz, seq, _ = proj_b.shape
    c = GDN_CHUNK
    assert c == LANE and seq % c == 0
    tril3, levels = _gdn_constants()
    blk = lambda col: pl.BlockSpec((1, c, D_MODEL), lambda b, s: (b, s, col))
    return pl.pallas_call(
        _gdn_kernel,
        grid=(bsz, seq // c),
        in_specs=[
            blk(0), blk(1), blk(2), blk(3),
            pl.BlockSpec((1, c, LANE), lambda b, s: (b, s, 0)),
            pl.BlockSpec((1, LANE), lambda b, s: (0, 0)),
            pl.BlockSpec((1, LANE), lambda b, s: (0, 0)),
            pl.BlockSpec((1, HEAD_W), lambda b, s: (0, 0)),
            _resident(tril3.shape), _resident(levels.shape),
        ],
        out_specs=pl.BlockSpec((1, c, D_MODEL), lambda b, s: (b, s, 0)),
        out_shape=jax.ShapeDtypeStruct((bsz, seq, D_MODEL), BF16),
        scratch_shapes=[pltpu.VMEM((N_HEADS, DK_B, HEAD_W), F32)],
        compiler_params=_params(("parallel", "arbitrary")),
        name="gated_delta_rule",
    )(proj_b, proj_b, proj_b, proj_b, small, alog_row, dtb_row, onorm, tril3, levels)


def _merge_kernel(x_ref, oa_ref, ob_ref, ga_ref, gb_ref, wa_ref, wb_ref, wo_ref, o_ref):
    ya = _dot(oa_ref[...], wa_ref[...])
    yb = _dot(ob_ref[...], wb_ref[...])
    mixed = _sigmoid(ga_ref[...].astype(F32)) * ya + _sigmoid(gb_ref[...].astype(F32)) * yb
    o_ref[...] = x_ref[...] + _dot(mixed.astype(BF16), wo_ref[...])


def _merge(x2, oa, ob, proj_b, wa, wb, wo):
    rows = x2.shape[0]
    gate_block = 4
    row_spec = pl.BlockSpec((MERGE_ROWS, D_MODEL), lambda i: (i, 0))
    return pl.pallas_call(
        _merge_kernel,
        grid=(rows // MERGE_ROWS,),
        in_specs=[
            row_spec, row_spec, row_spec,
            pl.BlockSpec((MERGE_ROWS, D_MODEL), lambda i: (i, gate_block)),
            pl.BlockSpec((MERGE_ROWS, D_MODEL), lambda i: (i, gate_block + 1)),
            _resident((D_MODEL, D_MODEL)), _resident((D_MODEL, D_MODEL)), _resident((D_MODEL, D_MODEL)),
        ],
        out_specs=row_spec,
        out_shape=jax.ShapeDtypeStruct((rows, D_MODEL), F32),
        compiler_params=_params(("parallel",)),
        name="branch_merge",
    )(x2, oa, ob, proj_b, proj_b, wa, wb, wo)


def _ffn_ple_kernel(x_ref, halo_ref, p_ref, fg_ref, wup_ref, cw_ref, cb_ref, wdn_ref, pg_ref, wg_ref, wp_ref,
                    o_ref, act_ref, *, seq):
    rows = x_ref.shape[0]
    x = x_ref[...]
    fg = fg_ref[...]
    keep = jnp.where((pl.program_id(0) * rows) % seq == 0, 0.0, 1.0)
    h_ext = jnp.concatenate([_rmsnorm(halo_ref[...], fg) * keep, _rmsnorm(x, fg)], axis=0).astype(BF16)
    off = HALO - (CONV_FF - 1)

    def conv(u, lo):
        last = CONV_FF - 1
        y = cb_ref[:, lo:lo + FFN_COLS] + cw_ref[last:CONV_FF, lo:lo + FFN_COLS] * u[off + last:off + last + rows]
        for tap in range(last):
            y = y + cw_ref[tap:tap + 1, lo:lo + FFN_COLS] * u[off + tap:off + tap + rows]
        return y

    for c in range(D_FF // FFN_COLS):
        lo = c * FFN_COLS
        yg = conv(_dot(h_ext, wup_ref[:, lo:lo + FFN_COLS]), lo)
        yv = conv(_dot(h_ext, wup_ref[:, D_FF + lo:D_FF + lo + FFN_COLS]), D_FF + lo)
        act_ref[:, lo:lo + FFN_COLS] = (_silu(yg) * yv).astype(BF16)

    x1 = x + _dot(act_ref[...], wdn_ref[...])
    h2 = _rmsnorm(x1, pg_ref[...]).astype(BF16)
    gate = _sigmoid(_dot(h2, wg_ref[...]))
    o_ref[...] = x1 + gate * _dot(p_ref[...].astype(BF16), wp_ref[...])


def _ffn_ple(x2, p2, fg, wup, cw, cb, wdn, pg, wg, wp, seq):
    rows = x2.shape[0]
    assert seq % FFN_ROWS == 0 and D_FF % FFN_COLS == 0
    halo_blocks = FFN_ROWS // HALO
    return pl.pallas_call(
        functools.partial(_ffn_ple_kernel, seq=seq),
        grid=(rows // FFN_ROWS,),
        in_specs=[
            pl.BlockSpec((FFN_ROWS, D_MODEL), lambda i: (i, 0)),
            pl.BlockSpec((HALO, D_MODEL), lambda i: (jnp.maximum(i * halo_blocks - 1, 0), 0)),
            pl.BlockSpec((FFN_ROWS, D_PLE), lambda i: (i, 0)),
            _resident((1, D_MODEL)),
            _resident((D_MODEL, 2 * D_FF)),
            _resident((CONV_FF, 2 * D_FF)),
            _resident((1, 2 * D_FF)),
            _resident((D_FF, D_MODEL)),
            _resident((1, D_MODEL)),
            _resident((D_MODEL, D_MODEL)),
            _resident((D_PLE, D_MODEL)),
        ],
        out_specs=pl.BlockSpec((FFN_ROWS, D_MODEL), lambda i: (i, 0)),
        out_shape=jax.ShapeDtypeStruct((rows, D_MODEL), F32),
        scratch_shapes=[pltpu.VMEM((FFN_ROWS, D_FF), BF16)],
        compiler_params=_params(("arbitrary",)),
        name="conv_ffn_ple",
    )(x2, x2, p2, fg, wup, cw, cb, wdn, pg, wg, wp)


def kernel(x, p, w_in, attn_norm, qnorm_a, knorm_a, lam_a, subln_a, rel_bias, conv_b, a_log_b, dt_bias_b, onorm_b, w_br_a, w_br_b, w_out, ffn_norm, w_up, conv_ff, conv_ff_bias, w_down, ple_norm, w_ple_gate, w_ple_proj):
    bsz, seq, _ = x.shape
    depth = w_in.shape[0]
    rows = bsz * seq
    assert rows % PROJ_ROWS == 0 and seq % ATT_BLOCK == 0 and rows % MERGE_ROWS == 0

    n_a = 3 * D_MODEL
    n_b = 4 * D_MODEL
    w_in_b = w_in.astype(BF16)
    rel_bias_t = rel_bias.astype(F32).T * LOG2E
    bias_diag, bias_corner = _bias_tiles(rel_bias_t)
    n_maps = D_MODEL // HEAD_DIM_A

    x2 = x.reshape(rows, D_MODEL).astype(F32)
    for i in range(depth):
        lam_init = 0.8 - 0.6 * math.exp(-0.3 * i)
        w_a = w_in_b[i, :, 0:n_a]
        w_b = jnp.concatenate([w_in_b[i, :, n_a:n_a + n_b], w_in_b[i, :, n_a + n_b + 2 * N_HEADS:]], axis=1)
        w_small = jnp.pad(w_in_b[i, :, n_a + n_b:n_a + n_b + 2 * N_HEADS], ((0, 0), (0, LANE - 2 * N_HEADS)))
        gains = jnp.stack([
            jnp.tile(qnorm_a[i].astype(F32), n_maps) * (HEAD_DIM_A ** -0.5 * LOG2E),
            jnp.tile(knorm_a[i].astype(F32), n_maps),
            jnp.ones((D_MODEL,), F32),
        ]).reshape(3, 1, D_MODEL)
        g_attn = attn_norm[i].astype(F32).reshape(1, D_MODEL)

        qkv = _attn_proj(x2, g_attn, w_a, gains)
        proj_b, small = _gdn_proj(x2, g_attn, w_b, w_small, conv_b[i].astype(F32), seq)

        oa = _attention(qkv.reshape(bsz, seq, n_a), bias_diag, bias_corner, rel_bias_t, lam_a[i].astype(F32),
                        subln_a[i].astype(F32).reshape(1, HEAD_W), jnp.full((1,), lam_init, F32))
        head_lanes = (N_HEADS, LANE - 2 * N_HEADS)
        ob = _gdn(proj_b.reshape(bsz, seq, -1), small.reshape(bsz, seq, LANE),
                  jnp.pad(a_log_b[i].astype(F32), head_lanes).reshape(1, LANE),
                  jnp.pad(dt_bias_b[i].astype(F32), head_lanes).reshape(1, LANE),
                  onorm_b[i].astype(F32).reshape(1, HEAD_W))

        x2 = _merge(x2, oa.reshape(rows, D_MODEL), ob.reshape(rows, D_MODEL), proj_b,
                    w_br_a[i].astype(BF16), w_br_b[i].astype(BF16), w_out[i].astype(BF16))

        x2 = _ffn_ple(x2, p[i].reshape(rows, D_PLE), ffn_norm[i].astype(F32).reshape(1, D_MODEL),
                      w_up[i].astype(BF16), conv_ff[i].astype(F32), conv_ff_bias[i].astype(F32).reshape(1, 2 * D_FF),
                      w_down[i].astype(BF16), ple_norm[i].astype(F32).reshape(1, D_MODEL),
                      w_ple_gate[i].astype(BF16), w_ple_proj[i].astype(BF16), seq)
    return x2.reshape(bsz, seq, D_MODEL).astype(x.dtype)
```

```python
import functools
import math

import numpy as np
import jax
import jax.numpy as jnp
from jax import lax
from jax.experimental import pallas as pl
from jax.experimental.pallas import tpu as pltpu

F32 = jnp.float32
BF16 = jnp.bfloat16

D_MODEL = 1024
N_HEADS = 8
HEAD_W = 128
HEAD_DIM_A = 64
DK_B = 128
CONV_B = 4
CHUNK_B = 64
D_FF = 2816
CONV_FF = 3
D_PLE = 256
N_BUCKETS = 32
MAX_DISTANCE = 128
EPS = 1e-6
NEG_INF = -1e30

LANE = 128
SUBLANE = 8
VMEM_LIMIT_BYTES = 56 * 1024 * 1024

PROJ_ROWS = 1024
PROJ_COLS = 1024
PROJ_SUB = 256
ATT_BLOCK = 512
ATT_HEADS = 8
ATT_ONES = 2 * SUBLANE
ATT_SKEW = 1
ATT_CORNER = 128
LOG2E = math.log2(math.e)
GDN_CHUNK = 128
MERGE_ROWS = 512
FFN_ROWS = 512
FFN_COLS = 256
HALO = SUBLANE
PROJ_HALO = 2 * SUBLANE


def _bucket_thresholds():
    n = np.arange(0, 4 * MAX_DISTANCE)
    max_exact = N_BUCKETS // 2
    nf = np.maximum(n, 1).astype(np.float32)
    large = max_exact + (np.log(nf / np.float32(max_exact)) / np.float32(math.log(MAX_DISTANCE / max_exact))
                         * np.float32(N_BUCKETS - max_exact)).astype(np.int32)
    bucket = np.where(n < max_exact, n, np.minimum(large, N_BUCKETS - 1))
    return [int(np.argmax(bucket >= b)) for b in range(1, N_BUCKETS)]


BUCKET_LO = _bucket_thresholds()


def _params(semantics):
    return pltpu.CompilerParams(dimension_semantics=semantics, vmem_limit_bytes=VMEM_LIMIT_BYTES)


def _resident(shape):
    zeros = (0,) * len(shape)
    return pl.BlockSpec(shape, lambda *_: zeros, pipeline_mode=pl.Buffered(1))


def _rmsnorm(x, g):
    ms = jnp.mean(x * x, axis=-1, keepdims=True)
    return x * lax.rsqrt(ms + EPS) * g


def _sigmoid(x):
    return 1.0 / (1.0 + jnp.exp2(x * (-LOG2E)))


def _silu(x):
    return x * _sigmoid(x)


def _dot(a, b):
    return jnp.dot(a, b, preferred_element_type=F32)


def _dot_nt(a, b):
    return lax.dot_general(a, b, (((1,), (1,)), ((), ())), preferred_element_type=F32)


def _dot_tn(a, b):
    return lax.dot_general(a, b, (((0,), (0,)), ((), ())), preferred_element_type=F32)


def _attn_proj_kernel(x_ref, g_ref, w_ref, gain_ref, gsum_ref, gexp_ref, o_ref, h_ref):
    j = pl.program_id(1)

    @pl.when(j == 0)
    def _():
        h_ref[...] = _rmsnorm(x_ref[...], g_ref[...]).astype(BF16)

    @pl.when(j < 2)
    def _():
        n_sub = PROJ_COLS // PROJ_SUB

        def project(sub):
            return _dot(h_ref[...], w_ref[:, sub * PROJ_SUB:(sub + 1) * PROJ_SUB])

        def normalise(sub, y):
            cols = slice(sub * PROJ_SUB, (sub + 1) * PROJ_SUB)
            ss = _dot((y * y).astype(BF16), gsum_ref[cols, :])
            r = lax.rsqrt(ss * (1.0 / HEAD_DIM_A) + EPS)
            r_hi = r.astype(BF16)
            r_lo = (r - r_hi.astype(F32)).astype(BF16)
            rb = _dot(jnp.concatenate([r_hi, r_lo], axis=1), gexp_ref[:, cols])
            o_ref[:, cols] = (y * rb * gain_ref[:, cols]).astype(BF16)

        y = project(0)
        for sub in range(n_sub):
            y_next = project(sub + 1) if sub + 1 < n_sub else None
            normalise(sub, y)
            y = y_next

    @pl.when(j == 2)
    def _():
        o_ref[...] = _dot(h_ref[...], w_ref[...]).astype(BF16)


def _attn_proj(x2, g, w, gains):
    rows = x2.shape[0]
    n_groups = D_MODEL // HEAD_DIM_A
    col = np.arange(D_MODEL) // HEAD_DIM_A
    gsum = (col[:, None] == np.arange(LANE)[None, :]).astype(np.float32)
    gexp = np.concatenate([gsum.T, gsum.T], axis=0)
    assert n_groups <= LANE
    return pl.pallas_call(
        _attn_proj_kernel,
        grid=(rows // PROJ_ROWS, 3),
        in_specs=[
            pl.BlockSpec((PROJ_ROWS, D_MODEL), lambda i, j: (i, 0)),
            pl.BlockSpec((1, D_MODEL), lambda i, j: (0, 0)),
            pl.BlockSpec((D_MODEL, PROJ_COLS), lambda i, j: (0, j)),
            pl.BlockSpec((None, 1, PROJ_COLS), lambda i, j: (j, 0, 0)),
            pl.BlockSpec((D_MODEL, LANE), lambda i, j: (0, 0)),
            pl.BlockSpec((2 * LANE, D_MODEL), lambda i, j: (0, 0)),
        ],
        out_specs=pl.BlockSpec((PROJ_ROWS, PROJ_COLS), lambda i, j: (i, j)),
        out_shape=jax.ShapeDtypeStruct((rows, 3 * D_MODEL), BF16),
        scratch_shapes=[pltpu.VMEM((PROJ_ROWS, D_MODEL), BF16)],
        compiler_params=_params(("parallel", "arbitrary")),
        name="attn_proj",
    )(x2, g, w, gains, jnp.asarray(gsum, BF16), jnp.asarray(gexp, BF16))


def _gdn_proj_kernel(x_ref, halo_ref, g_ref, w_ref, wsm_ref, cw_ref, o_ref, osm_ref, h_ref, u_ref, *, seq):
    i = pl.program_id(0)
    j = pl.program_id(1)
    rows = x_ref.shape[0]

    @pl.when(j == 0)
    def _():
        g = g_ref[...]
        keep = jnp.where((i * rows) % seq == 0, 0.0, 1.0)
        h_ref[0:PROJ_HALO, :] = (_rmsnorm(halo_ref[...], g) * keep).astype(BF16)
        h = _rmsnorm(x_ref[...], g).astype(BF16)
        h_ref[PROJ_HALO:, :] = h
        osm_ref[...] = _dot(h, wsm_ref[...])

    @pl.when(j < 3)
    def _():
        off = PROJ_HALO - (CONV_B - 1)
        scale = jnp.where(j == 0, DK_B ** -0.5, 1.0)

        def project(sub):
            cols = slice(sub * PROJ_SUB, (sub + 1) * PROJ_SUB)
            u_ref[:, cols] = _dot(h_ref[...], w_ref[:, cols])

        def epilogue(sub):
            for hh in range(PROJ_SUB // HEAD_W):
                lo = sub * PROJ_SUB + hh * HEAD_W
                last = CONV_B - 1
                y = cw_ref[last:CONV_B, lo:lo + HEAD_W] * u_ref[off + last:off + last + rows, lo:lo + HEAD_W]
                for tap in range(last):
                    y = y + cw_ref[tap:tap + 1, lo:lo + HEAD_W] * u_ref[off + tap:off + tap + rows, lo:lo + HEAD_W]
                y = _silu(y)
                r = lax.rsqrt(jnp.sum(y * y, axis=-1, keepdims=True) + EPS) * scale
                o_ref[:, lo:lo + HEAD_W] = (y * jnp.where(j == 2, 1.0, r)).astype(BF16)

        n_sub = PROJ_COLS // PROJ_SUB
        project(0)
        for sub in range(n_sub):
            if sub + 1 < n_sub:
                project(sub + 1)
            epilogue(sub)

    @pl.when(j >= 3)
    def _():
        o_ref[...] = _dot(h_ref[PROJ_HALO:, :], w_ref[...]).astype(BF16)


def _gdn_proj(x2, g, w, wsm, conv_w, seq):
    rows = x2.shape[0]
    n_cols = w.shape[1]
    assert seq % PROJ_ROWS == 0 and PROJ_ROWS % PROJ_HALO == 0
    halo_blocks = PROJ_ROWS // PROJ_HALO
    return pl.pallas_call(
        functools.partial(_gdn_proj_kernel, seq=seq),
        grid=(rows // PROJ_ROWS, n_cols // PROJ_COLS),
        in_specs=[
            pl.BlockSpec((PROJ_ROWS, D_MODEL), lambda i, j: (i, 0)),
            pl.BlockSpec((PROJ_HALO, D_MODEL), lambda i, j: (jnp.maximum(i * halo_blocks - 1, 0), 0)),
            pl.BlockSpec((1, D_MODEL), lambda i, j: (0, 0)),
            pl.BlockSpec((D_MODEL, PROJ_COLS), lambda i, j: (0, j)),
            pl.BlockSpec((D_MODEL, LANE), lambda i, j: (0, 0)),
            pl.BlockSpec((CONV_B, PROJ_COLS), lambda i, j: (0, jnp.minimum(j, 2))),
        ],
        out_specs=[
            pl.BlockSpec((PROJ_ROWS, PROJ_COLS), lambda i, j: (i, j)),
            pl.BlockSpec((PROJ_ROWS, LANE), lambda i, j: (i, 0)),
        ],
        out_shape=[
            jax.ShapeDtypeStruct((rows, n_cols), BF16),
            jax.ShapeDtypeStruct((rows, LANE), F32),
        ],
        scratch_shapes=[pltpu.VMEM((PROJ_HALO + PROJ_ROWS, D_MODEL), BF16),
                        pltpu.VMEM((PROJ_HALO + PROJ_ROWS, PROJ_COLS), F32)],
        compiler_params=_params(("parallel", "arbitrary")),
        name="gdn_proj",
    )(x2, x2, g, w, wsm, conv_w)


def _bias_tiles_kernel(tab_ref, diag_ref, corner_ref):
    h = pl.program_id(0)
    t = diag_ref.shape[1]
    cn = corner_ref.shape[1]

    def bias_of(rel):
        b = jnp.full(rel.shape, tab_ref[h, 0], F32)
        for bk in range(1, N_BUCKETS):
            b = jnp.where(rel >= BUCKET_LO[bk - 1], tab_ref[h, bk], b)
        return b

    rel = lax.broadcasted_iota(jnp.int32, (t, t), 1) - lax.broadcasted_iota(jnp.int32, (t, t), 0)
    diag_ref[0] = jnp.where(rel >= 0, bias_of(rel), NEG_INF)
    rel_c = (lax.broadcasted_iota(jnp.int32, (cn, cn), 1) - lax.broadcasted_iota(jnp.int32, (cn, cn), 0)) + cn
    corner_ref[0] = bias_of(rel_c) - tab_ref[h, N_BUCKETS - 1]


def _bias_tiles(rel_bias_t):
    t = ATT_BLOCK
    assert ATT_CORNER + 1 >= BUCKET_LO[-1], "entries outside the corner must sit in the last bucket"
    return pl.pallas_call(
        _bias_tiles_kernel,
        grid=(N_HEADS,),
        in_specs=[pl.BlockSpec(memory_space=pltpu.SMEM)],
        out_specs=[
            pl.BlockSpec((1, t, t), lambda h: (h, 0, 0)),
            pl.BlockSpec((1, ATT_CORNER, ATT_CORNER), lambda h: (h, 0, 0)),
        ],
        out_shape=[
            jax.ShapeDtypeStruct((N_HEADS, t, t), F32),
            jax.ShapeDtypeStruct((N_HEADS, ATT_CORNER, ATT_CORNER), F32),
        ],
        compiler_params=_params(("parallel",)),
        name="bias_tiles",
    )(rel_bias_t)


def _attn_kernel(qi_tab, kj_tab, q_ref, k_ref, v_ref, diag_ref, corner_ref, tab_ref, lam_ref, sg_ref, li_ref,
                 o_ref, qs_ref, vt_ref, m_ref, acc_ref):
    hg = pl.program_id(0)
    step = pl.program_id(2)
    qi = qi_tab[step]
    kj = kj_tab[step]
    t = q_ref.shape[1]
    cn = corner_ref.shape[1]
    heads = range(ATT_HEADS)
    cols = [slice(g * HEAD_W, (g + 1) * HEAD_W) for g in heads]

    @pl.when(kj == 0)
    def _():
        lane = lax.broadcasted_iota(jnp.int32, (t, HEAD_W), 1)
        for g in heads:
            q = q_ref[0, :, cols[g]]
            zero = jnp.zeros_like(q)
            qs_ref[g, 0] = jnp.where(lane < HEAD_DIM_A, q, zero)
            qs_ref[g, 1] = jnp.where(lane >= HEAD_DIM_A, q, zero)
            vt_ref[g, HEAD_W:HEAD_W + ATT_ONES, :] = jnp.ones((ATT_ONES, t), BF16)
        m_ref[...] = jnp.full(m_ref.shape, NEG_INF, F32)
        acc_ref[...] = jnp.zeros(acc_ref.shape, F32)

    for g in heads:
        vt_ref[g, 0:HEAD_W, :] = v_ref[0, :, cols[g]].astype(F32).T.astype(BF16)
    delta = qi - kj

    def sweep(bias_fn, c):
        units = [(g, j) for g in heads for j in range(2)]
        n = len(units)
        s, m_prev, m_new, p = {}, {}, {}, {}

        def logits(i):
            g, j = units[i]
            s[i] = bias_fn(_dot_nt(k_ref[0, :, cols[g]], qs_ref[g, j]), g)

        def softmax(i):
            g, j = units[i]
            m_prev[i] = m_ref[g, j]
            col_max = jnp.broadcast_to(jnp.max(s[i], axis=0, keepdims=True), (SUBLANE, t))
            m_new[i] = jnp.maximum(m_prev[i], col_max + c[g])
            p[i] = jnp.exp2(s.pop(i) - (m_new[i][0:1, :] - c[g])).astype(BF16)

        def values(i):
            g, j = units[i]
            pv = _dot(vt_ref[g], p.pop(i))
            alpha = jnp.exp2(m_prev.pop(i) - m_new[i])
            acc_ref[g, j] = alpha[0:1, :] * acc_ref[g, j] + pv
            m_ref[g, j] = m_new.pop(i)

        for i in range(n + 2 * ATT_SKEW):
            if i < n:
                logits(i)
            if 0 <= i - ATT_SKEW < n:
                softmax(i - ATT_SKEW)
            if 0 <= i - 2 * ATT_SKEW < n:
                values(i - 2 * ATT_SKEW)

    @pl.when(delta >= 1)
    def _():
        near = jnp.where(delta == 1, 1.0, 0.0)

        def bias_fn(s, g):
            bottom = jnp.concatenate([s[t - cn:t, 0:cn] + near * corner_ref[g], s[t - cn:t, cn:t]], axis=1)
            return jnp.concatenate([s[0:t - cn, :], bottom], axis=0)

        sweep(bias_fn, [tab_ref[hg * ATT_HEADS + g, N_BUCKETS - 1] for g in heads])

    @pl.when(delta == 0)
    def _():
        sweep(lambda s, g: s + diag_ref[g], [0.0] * ATT_HEADS)
        lq = lam_ref[...]
        lam_init = li_ref[0]
        lam = (jnp.exp(jnp.sum(lq[0:1] * lq[1:2], axis=-1, keepdims=True))
               - jnp.exp(jnp.sum(lq[2:3] * lq[3:4], axis=-1, keepdims=True)) + lam_init)
        for g in heads:
            a0 = acc_ref[g, 0]
            a1 = acc_ref[g, 1]
            o_t = (a0[0:HEAD_W] / a0[HEAD_W:HEAD_W + 1] - lam * (a1[0:HEAD_W] / a1[HEAD_W:HEAD_W + 1]))
            o_ref[0, :, cols[g]] = (_rmsnorm(o_t.T, sg_ref[...]) * (1.0 - lam_init)).astype(BF16)


def _attention(qkv, bias_diag, bias_corner, rel_bias_t, lam_a, subln, lam_init):
    bsz, seq, _ = qkv.shape
    t = ATT_BLOCK
    nq = seq // t
    assert ATT_CORNER <= t and N_HEADS % ATT_HEADS == 0
    n_groups = N_HEADS // ATT_HEADS
    gw = ATT_HEADS * HEAD_W
    pairs =[(qi, kj) for qi in range(nq) for kj in range(qi + 1)]
    qi_tab = jnp.asarray([p[0] for p in pairs], jnp.int32)
    kj_tab = jnp.asarray([p[1] for p in pairs], jnp.int32)
    grid_spec = pltpu.PrefetchScalarGridSpec(
        num_scalar_prefetch=2,
        grid=(n_groups, bsz, len(pairs)),
        in_specs=[
            pl.BlockSpec((1, t, gw), lambda h, b, s, qt, kt: (b, qt[s], h)),
            pl.BlockSpec((1, t, gw), lambda h, b, s, qt, kt: (b, kt[s], n_groups + h)),
            pl.BlockSpec((1, t, gw), lambda h, b, s, qt, kt: (b, kt[s], 2 * n_groups + h)),
            pl.BlockSpec((ATT_HEADS, t, t), lambda h, b, s, qt, kt: (h, 0, 0)),
            pl.BlockSpec((ATT_HEADS, ATT_CORNER, ATT_CORNER), lambda h, b, s, qt, kt: (h, 0, 0)),
            pl.BlockSpec(memory_space=pltpu.SMEM),
            pl.BlockSpec((4, HEAD_DIM_A), lambda h, b, s, qt, kt: (0, 0)),
            pl.BlockSpec((1, HEAD_W), lambda h, b, s, qt, kt: (0, 0)),
            pl.BlockSpec(memory_space=pltpu.SMEM),
        ],
        out_specs=pl.BlockSpec((1, t, gw), lambda h, b, s, qt, kt: (b, qt[s], h)),
        scratch_shapes=[
            pltpu.VMEM((ATT_HEADS, 2, t, HEAD_W), BF16),
            pltpu.VMEM((ATT_HEADS, HEAD_W + ATT_ONES, t), BF16),
            pltpu.VMEM((ATT_HEADS, 2, SUBLANE, t), F32),
            pltpu.VMEM((ATT_HEADS, 2, HEAD_W + ATT_ONES, t), F32),
        ],
    )
    return pl.pallas_call(
        _attn_kernel,
        grid_spec=grid_spec,
        out_shape=jax.ShapeDtypeStruct((bsz, seq, D_MODEL), BF16),
        compiler_params=_params(("parallel", "parallel", "arbitrary")),
        name="diff_attention",
    )(qi_tab, kj_tab, qkv, qkv, qkv, bias_diag, bias_corner, rel_bias_t, lam_a, subln, lam_init)


def _split(x, n_pieces):
    pieces = []
    for _ in range(n_pieces):
        piece = x.astype(BF16)
        pieces.append(piece)
        x = x - piece.astype(F32)
    return pieces


def _gdn_kernel(q_ref, k_ref, v_ref, z_ref, sm_ref, alog_ref, dtb_ref, on_ref, tril_ref, lvl_ref,
                o_ref, state_ref):
    c = GDN_CHUNK

    @pl.when(pl.program_id(1) == 0)
    def _():
        state_ref[...] = jnp.zeros(state_ref.shape, F32)

    sm = sm_ref[0]
    beta_all = _sigmoid(sm)
    xa = sm + dtb_ref[...]
    softplus = jnp.maximum(xa, 0.0) + jnp.log1p(jnp.exp(-jnp.abs(xa)))
    g_all = -jnp.exp(alog_ref[...]) * softplus
    gc_all = _dot(tril_ref[...], jnp.concatenate(_split(g_all, 3), axis=0))
    eg_all = jnp.exp(gc_all)
    ed_all = jnp.exp(gc_all[c - 1:c, :] - gc_all)

    ii = lax.broadcasted_iota(jnp.int32, (c, c), 0)
    jj = lax.broadcasted_iota(jnp.int32, (c, c), 1)
    tril = ii >= jj
    strict = ii > jj
    n_levels = lvl_ref.shape[0] - 1

    heads = range(N_HEADS)
    cols = [slice(h * HEAD_W, (h + 1) * HEAD_W) for h in heads]
    beta = [beta_all[:, h:h + 1] for h in heads]
    gcr = [jnp.broadcast_to(gc_all[:, N_HEADS + h:N_HEADS + h + 1], (c, c)) for h in heads]
    eg = [eg_all[:, N_HEADS + h:N_HEADS + h + 1] for h in heads]
    ed = [ed_all[:, N_HEADS + h:N_HEADS + h + 1] for h in heads]
    k = [k_ref[0, :, cols[h]] for h in heads]
    kq = [_dot_nt(jnp.concatenate([k[h], q_ref[0, :, cols[h]]], axis=0), k[h]) for h in heads]
    decay = [jnp.where(tril, jnp.exp(gcr[h] - gcr[h].T), 0.0) for h in heads]
    a_b = [jnp.where(strict, kq[h][0:c] * (decay[h] * beta[h]), 0.0).astype(BF16) for h in heads]
    attn_b = [(kq[h][c:2 * c] * decay[h]).astype(BF16) for h in heads]
    inv = [lvl_ref[0] - a_b[h] * lvl_ref[1] for h in heads]
    for lv in range(2, n_levels + 1):
        x = [_dot(a_b[h] * lvl_ref[lv], inv[h]).astype(BF16) for h in heads]
        inv = [inv[h] - _dot(inv[h], x[h]).astype(BF16) for h in heads]
    rhs = [jnp.concatenate([k[h] * (beta[h] * eg[h]).astype(BF16),
                            v_ref[0, :, cols[h]] * beta[h].astype(BF16)], axis=1) for h in heads]
    wu = [_dot(inv[h], rhs[h]).astype(BF16) for h in heads]
    aw = [_dot(attn_b[h], wu[h]) for h in heads]
    pn = [_dot_tn(k[h] * ed[h].astype(BF16), wu[h]) for h in heads]
    q_eff = [q_ref[0, :, cols[h]].astype(F32) * eg[h] - aw[h][:, 0:HEAD_W] for h in heads]
    state = [state_ref[h] for h in heads]
    r = [_dot(jnp.concatenate([pn[h][:, 0:HEAD_W], q_eff[h]], axis=0).astype(BF16), state[h].astype(BF16))
         for h in heads]
    for h in heads:
        state_ref[h] = state[h] * eg[h][c - 1:c, :] + pn[h][:, HEAD_W:2 * HEAD_W] - r[h][0:DK_B]
    for h in heads:
        o = r[h][DK_B:DK_B + c] + aw[h][:, HEAD_W:2 * HEAD_W]
        z = z_ref[0, :, cols[h]].astype(F32)
        o_ref[0, :, cols[h]] = (_rmsnorm(o, on_ref[...]) * _silu(z)).astype(BF16)


def _gdn_constants():
    c = GDN_CHUNK
    r = np.arange(c)
    tril = (r[None, :] <= r[:, None]).astype(np.float32)
    tril3 = np.concatenate([tril, tril, tril], axis=1)
    levels = [np.eye(c, dtype=np.float32)]
    s = 1
    while s < c:
        same = (r[:, None] ^ r[None, :]) < 2 * s
        levels.append((same & ((r[:, None] & s) != 0) & ((r[None, :] & s) == 0)).astype(np.float32))
        s *= 2
    return jnp.asarray(tril3, BF16), jnp.asarray(np.stack(levels), BF16)


def _gdn(proj_b, small, alog_row, dtb_row, onorm):
    bsz, seq, _ = proj_b.shape
    c = GDN_CHUNK
    assert c == LANE and seq % c == 0
    tril3, levels = _gdn_constants()
    blk = lambda col: pl.BlockSpec((1, c, D_MODEL), lambda b, s: (b, s, col))
    return pl.pallas_call(
        _gdn_kernel,
        grid=(bsz, seq // c),
        in_specs=[
            blk(0), blk(1), blk(2), blk(3),
            pl.BlockSpec((1, c, LANE), lambda b, s: (b, s, 0)),
            pl.BlockSpec((1, LANE), lambda b, s: (0, 0)),
            pl.BlockSpec((1, LANE), lambda b, s: (0, 0)),
            pl.BlockSpec((1, HEAD_W), lambda b, s: (0, 0)),
            _resident(tril3.shape), _resident(levels.shape),
        ],
        out_specs=pl.BlockSpec((1, c, D_MODEL), lambda b, s: (b, s, 0)),
        out_shape=jax.ShapeDtypeStruct((bsz, seq, D_MODEL), BF16),
        scratch_shapes=[pltpu.VMEM((N_HEADS, DK_B, HEAD_W), F32)],
        compiler_params=_params(("parallel", "arbitrary")),
        name="gated_delta_rule",
    )(proj_b, proj_b, proj_b, proj_b, small, alog_row, dtb_row, onorm, tril3, levels)


def _merge_kernel(x_ref, oa_ref, ob_ref, ga_ref, gb_ref, wa_ref, wb_ref, wo_ref, o_ref):
    ya = _dot(oa_ref[...], wa_ref[...])
    yb = _dot(ob_ref[...], wb_ref[...])
    mixed = _sigmoid(ga_ref[...].astype(F32)) * ya + _sigmoid(gb_ref[...].astype(F32)) * yb
    o_ref[...] = x_ref[...] + _dot(mixed.astype(BF16), wo_ref[...])


def _merge(x2, oa, ob, proj_b, wa, wb, wo):
    rows = x2.shape[0]
    gate_block = 4
    row_spec = pl.BlockSpec((MERGE_ROWS, D_MODEL), lambda i: (i, 0))
    return pl.pallas_call(
        _merge_kernel,
        grid=(rows // MERGE_ROWS,),
        in_specs=[
            row_spec, row_spec, row_spec,
            pl.BlockSpec((MERGE_ROWS, D_MODEL), lambda i: (i, gate_block)),
            pl.BlockSpec((MERGE_ROWS, D_MODEL), lambda i: (i, gate_block + 1)),
            _resident((D_MODEL, D_MODEL)), _resident((D_MODEL, D_MODEL)), _resident((D_MODEL, D_MODEL)),
        ],
        out_specs=row_spec,
        out_shape=jax.ShapeDtypeStruct((rows, D_MODEL), F32),
        compiler_params=_params(("parallel",)),
        name="branch_merge",
    )(x2, oa, ob, proj_b, proj_b, wa, wb, wo)


def _ffn_ple_kernel(x_ref, halo_ref, p_ref, fg_ref, wup_ref, cw_ref, cb_ref, wdn_ref, pg_ref, wg_ref, wp_ref,
                    o_ref, act_ref, *, seq):
    rows = x_ref.shape[0]
    x = x_ref[...]
    fg = fg_ref[...]
    keep = jnp.where((pl.program_id(0) * rows) % seq == 0, 0.0, 1.0)
    h_ext = jnp.concatenate([_rmsnorm(halo_ref[...], fg) * keep, _rmsnorm(x, fg)], axis=0).astype(BF16)
    off = HALO - (CONV_FF - 1)

    def conv(u, lo):
        last = CONV_FF - 1
        y = cb_ref[:, lo:lo + FFN_COLS] + cw_ref[last:CONV_FF, lo:lo + FFN_COLS] * u[off + last:off + last + rows]
        for tap in range(last):
            y = y + cw_ref[tap:tap + 1, lo:lo + FFN_COLS] * u[off + tap:off + tap + rows]
        return y

    for c in range(D_FF // FFN_COLS):
        lo = c * FFN_COLS
        yg = conv(_dot(h_ext, wup_ref[:, lo:lo + FFN_COLS]), lo)
        yv = conv(_dot(h_ext, wup_ref[:, D_FF + lo:D_FF + lo + FFN_COLS]), D_FF + lo)
        act_ref[:, lo:lo + FFN_COLS] = (_silu(yg) * yv).astype(BF16)

    x1 = x + _dot(act_ref[...], wdn_ref[...])
    h2 = _rmsnorm(x1, pg_ref[...]).astype(BF16)
    gate = _sigmoid(_dot(h2, wg_ref[...]))
    o_ref[...] = x1 + gate * _dot(p_ref[...].astype(BF16), wp_ref[...])


def _ffn_ple(x2, p2, fg, wup, cw, cb, wdn, pg, wg, wp, seq):
    rows = x2.shape[0]
    assert seq % FFN_ROWS == 0 and D_FF % FFN_COLS == 0
    halo_blocks = FFN_ROWS // HALO
    return pl.pallas_call(
        functools.partial(_ffn_ple_kernel, seq=seq),
        grid=(rows // FFN_ROWS,),
        in_specs=[
            pl.BlockSpec((FFN_ROWS, D_MODEL), lambda i: (i, 0)),
            pl.BlockSpec((HALO, D_MODEL), lambda i: (jnp.maximum(i * halo_blocks - 1, 0), 0)),
            pl.BlockSpec((FFN_ROWS, D_PLE), lambda i: (i, 0)),
            _resident((1, D_MODEL)),
            _resident((D_MODEL, 2 * D_FF)),
            _resident((CONV_FF, 2 * D_FF)),
            _resident((1, 2 * D_FF)),
            _resident((D_FF, D_MODEL)),
            _resident((1, D_MODEL)),
            _resident((D_MODEL, D_MODEL)),
            _resident((D_PLE, D_MODEL)),
        ],
        out_specs=pl.BlockSpec((FFN_ROWS, D_MODEL), lambda i: (i, 0)),
        out_shape=jax.ShapeDtypeStruct((rows, D_MODEL), F32),
        scratch_shapes=[pltpu.VMEM((FFN_ROWS, D_FF), BF16)],
        compiler_params=_params(("arbitrary",)),
        name="conv_ffn_ple",
    )(x2, x2, p2, fg, wup, cw, cb, wdn, pg, wg, wp)


def kernel(x, p, w_in, attn_norm, qnorm_a, knorm_a, lam_a, subln_a, rel_bias, conv_b, a_log_b, dt_bias_b, onorm_b, w_br_a, w_br_b, w_out, ffn_norm, w_up, conv_ff, conv_ff_bias, w_down, ple_norm, w_ple_gate, w_ple_proj):
    bsz, seq, _ = x.shape
    depth = w_in.shape[0]
    rows = bsz * seq
    assert rows % PROJ_ROWS == 0 and seq % ATT_BLOCK == 0 and rows % MERGE_ROWS == 0

    n_a = 3 * D_MODEL
    n_b = 4 * D_MODEL
    w_in_b = w_in.astype(BF16)
    rel_bias_t = rel_bias.astype(F32).T * LOG2E
    bias_diag, bias_corner = _bias_tiles(rel_bias_t)
    n_maps = D_MODEL // HEAD_DIM_A

    x2 = x.reshape(rows, D_MODEL).astype(F32)
    for i in range(depth):
        lam_init = 0.8 - 0.6 * math.exp(-0.3 * i)
        w_a = w_in_b[i, :, 0:n_a]
        w_b = jnp.concatenate([w_in_b[i, :, n_a:n_a + n_b], w_in_b[i, :, n_a + n_b + 2 * N_HEADS:]], axis=1)
        w_small = jnp.pad(w_in_b[i, :, n_a + n_b:n_a + n_b + 2 * N_HEADS], ((0, 0), (0, LANE - 2 * N_HEADS)))
        gains = jnp.stack([
            jnp.tile(qnorm_a[i].astype(F32), n_maps) * (HEAD_DIM_A ** -0.5 * LOG2E),
            jnp.tile(knorm_a[i].astype(F32), n_maps),
            jnp.ones((D_MODEL,), F32),
        ]).reshape(3, 1, D_MODEL)
        g_attn = attn_norm[i].astype(F32).reshape(1, D_MODEL)

        qkv = _attn_proj(x2, g_attn, w_a, gains)
        proj_b, small = _gdn_proj(x2, g_attn, w_b, w_small, conv_b[i].astype(F32), seq)

        oa = _attention(qkv.reshape(bsz, seq, n_a), bias_diag, bias_corner, rel_bias_t, lam_a[i].astype(F32),
                        subln_a[i].astype(F32).reshape(1, HEAD_W), jnp.full((1,), lam_init, F32))
        head_lanes = (N_HEADS, LANE - 2 * N_HEADS)
        ob = _gdn(proj_b.reshape(bsz, seq, -1), small.reshape(bsz, seq, LANE),
                  jnp.pad(a_log_b[i].astype(F32), head_lanes).reshape(1, LANE),
                  jnp.pad(dt_bias_b[i].astype(F32), head_lanes).reshape(1, LANE),
                  onorm_b[i].astype(F32).reshape(1, HEAD_W))

        x2 = _merge(x2, oa.reshape(rows, D_MODEL), ob.reshape(rows, D_MODEL), proj_b,
                    w_br_a[i].astype(BF16), w_br_b[i].astype(BF16), w_out[i].astype(BF16))

        x2 = _ffn_ple(x2, p[i].reshape(rows, D_PLE), ffn_norm[i].astype(F32).reshape(1, D_MODEL),
                      w_up[i].astype(BF16), conv_ff[i].astype(F32), conv_ff_bias[i].astype(F32).reshape(1, 2 * D_FF),
                      w_down[i].astype(BF16), ple_norm[i].astype(F32).reshape(1, D_MODEL),
                      w_ple_gate[i].astype(BF16), w_ple_proj[i].astype(BF16), seq)
    return x2.reshape(bsz, seq, D_MODEL).astype(x.dtype)
```

```python
import functools
import math

import numpy as np
import jax
import jax.numpy as jnp
from jax import lax
from jax.experimental import pallas as pl
from jax.experimental.pallas import tpu as pltpu

F32 = jnp.float32
BF16 = jnp.bfloat16

D_MODEL = 1024
N_HEADS = 8
HEAD_W = 128
HEAD_DIM_A = 64
DK_B = 128
CONV_B = 4
CHUNK_B = 64
D_FF = 2816
CONV_FF = 3
D_PLE = 256
N_BUCKETS = 32
MAX_DISTANCE = 128
EPS = 1e-6
NEG_INF = -1e30

LANE = 128
SUBLANE = 8
VMEM_LIMIT_BYTES = 56 * 1024 * 1024

PROJ_ROWS = 1024
PROJ_COLS = 1024
PROJ_SUB = 256
GDN_PROJ_ROWS = 512
ATT_BLOCK = 512
ATT_HEADS = 8
ATT_ONES = 2 * SUBLANE
ATT_SKEW = 1
ATT_CORNER = 128
LOG2E = math.log2(math.e)
GDN_CHUNK = 128
MERGE_ROWS = 512
FFN_ROWS = 512
FFN_COLS = 256
HALO = SUBLANE
PROJ_HALO = 2 * SUBLANE


def _bucket_thresholds():
    n = np.arange(0, 4 * MAX_DISTANCE)
    max_exact = N_BUCKETS // 2
    nf = np.maximum(n, 1).astype(np.float32)
    large = max_exact + (np.log(nf / np.float32(max_exact)) / np.float32(math.log(MAX_DISTANCE / max_exact))
                         * np.float32(N_BUCKETS - max_exact)).astype(np.int32)
    bucket = np.where(n < max_exact, n, np.minimum(large, N_BUCKETS - 1))
    return [int(np.argmax(bucket >= b)) for b in range(1, N_BUCKETS)]


BUCKET_LO = _bucket_thresholds()


def _params(semantics):
    return pltpu.CompilerParams(dimension_semantics=semantics, vmem_limit_bytes=VMEM_LIMIT_BYTES)


def _resident(shape):
    zeros = (0,) * len(shape)
    return pl.BlockSpec(shape, lambda *_: zeros, pipeline_mode=pl.Buffered(1))


def _rmsnorm(x, g):
    ms = jnp.mean(x * x, axis=-1, keepdims=True)
    return x * lax.rsqrt(ms + EPS) * g


def _sigmoid(x):
    return 1.0 / (1.0 + jnp.exp2(x * (-LOG2E)))


def _silu(x):
    return x * _sigmoid(x)


def _dot(a, b):
    return jnp.dot(a, b, preferred_element_type=F32)


def _dot_nt(a, b):
    return lax.dot_general(a, b, (((1,), (1,)), ((), ())), preferred_element_type=F32)


def _dot_tn(a, b):
    return lax.dot_general(a, b, (((0,), (0,)), ((), ())), preferred_element_type=F32)


def _attn_proj_kernel(x_ref, g_ref, w_ref, gain_ref, gsum_ref, o_ref, h_ref):
    j = pl.program_id(1)

    @pl.when(j == 0)
    def _():
        h_ref[...] = _rmsnorm(x_ref[...], g_ref[...]).astype(BF16)

    @pl.when(j < 2)
    def _():
        n_sub = PROJ_COLS // PROJ_SUB

        def project(sub):
            return _dot(h_ref[...], w_ref[:, sub * PROJ_SUB:(sub + 1) * PROJ_SUB])

        def normalise(sub, y):
            cols = slice(sub * PROJ_SUB, (sub + 1) * PROJ_SUB)
            ss = _dot((y * y).astype(BF16), gsum_ref[...])
            r = lax.rsqrt(ss * (1.0 / HEAD_DIM_A) + EPS)
            o_ref[:, cols] = (y * r * gain_ref[:, cols]).astype(BF16)

        y = project(0)
        for sub in range(n_sub):
            y_next = project(sub + 1) if sub + 1 < n_sub else None
            normalise(sub, y)
            y = y_next

    @pl.when(j == 2)
    def _():
        o_ref[...] = _dot(h_ref[...], w_ref[...]).astype(BF16)


def _attn_proj(x2, g, w, gains):
    rows = x2.shape[0]
    assert PROJ_SUB % HEAD_DIM_A == 0
    group = np.arange(PROJ_SUB) // HEAD_DIM_A
    gsum = (group[:, None] == group[None, :]).astype(np.float32)
    return pl.pallas_call(
        _attn_proj_kernel,
        grid=(rows // PROJ_ROWS, 3),
        in_specs=[
            pl.BlockSpec((PROJ_ROWS, D_MODEL), lambda i, j: (i, 0)),
            pl.BlockSpec((1, D_MODEL), lambda i, j: (0, 0)),
            pl.BlockSpec((D_MODEL, PROJ_COLS), lambda i, j: (0, j)),
            pl.BlockSpec((None, 1, PROJ_COLS), lambda i, j: (j, 0, 0)),
            pl.BlockSpec((PROJ_SUB, PROJ_SUB), lambda i, j: (0, 0)),
        ],
        out_specs=pl.BlockSpec((PROJ_ROWS, PROJ_COLS), lambda i, j: (i, j)),
        out_shape=jax.ShapeDtypeStruct((rows, 3 * D_MODEL), BF16),
        scratch_shapes=[pltpu.VMEM((PROJ_ROWS, D_MODEL), BF16)],
        compiler_params=_params(("parallel", "arbitrary")),
        name="attn_proj",
    )(x2, g, w, gains, jnp.asarray(gsum, BF16))


def _gdn_proj_kernel(x_ref, halo_ref, g_ref, w_ref, wsm_ref, cw_ref, o_ref, osm_ref, h_ref, u_ref, *, seq):
    rows = x_ref.shape[0]
    g = g_ref[...]
    keep = jnp.where((pl.program_id(0) * rows) % seq == 0, 0.0, 1.0)
    h_ref[0:PROJ_HALO, :] = (_rmsnorm(halo_ref[...], g) * keep).astype(BF16)
    h = _rmsnorm(x_ref[...], g).astype(BF16)
    h_ref[PROJ_HALO:, :] = h
    osm_ref[...] = _dot(h, wsm_ref[...])

    n_conv_cols = cw_ref.shape[1]
    n_conv = n_conv_cols // PROJ_SUB
    n_plain = (w_ref.shape[1] - n_conv_cols) // PROJ_SUB
    off = PROJ_HALO - (CONV_B - 1)

    def project_conv(sub):
        u_ref[sub % 2] = _dot(h_ref[...], w_ref[:, sub * PROJ_SUB:(sub + 1) * PROJ_SUB])

    def project_plain(sub):
        cols = slice(n_conv_cols + sub * PROJ_SUB, n_conv_cols + (sub + 1) * PROJ_SUB)
        o_ref[:, cols] = _dot(h_ref[PROJ_HALO:, :], w_ref[:, cols]).astype(BF16)

    def epilogue(sub):
        kind = sub * PROJ_SUB // D_MODEL
        u = u_ref.at[sub % 2]
        for hh in range(PROJ_SUB // HEAD_W):
            lo = hh * HEAD_W
            col = sub * PROJ_SUB + lo
            last = CONV_B - 1
            y = cw_ref[last:CONV_B, col:col + HEAD_W] * u[off + last:off + last + rows, lo:lo + HEAD_W]
            for tap in range(last):
                y = y + cw_ref[tap:tap + 1, col:col + HEAD_W] * u[off + tap:off + tap + rows, lo:lo + HEAD_W]
            y = _silu(y)
            if kind < 2:
                y = y * (lax.rsqrt(jnp.sum(y * y, axis=-1, keepdims=True) + EPS) * (DK_B ** -0.5 if kind == 0 else 1.0))
            o_ref[:, col:col + HEAD_W] = y.astype(BF16)

    project_conv(0)
    for sub in range(max(n_conv, n_plain)):
        if sub + 1 < n_conv:
            project_conv(sub + 1)
        if sub < n_plain:
            project_plain(sub)
        if sub < n_conv:
            epilogue(sub)


def _gdn_proj(x2, g, w, wsm, conv_w, seq):
    rows = x2.shape[0]
    n_cols = w.shape[1]
    assert seq % GDN_PROJ_ROWS == 0 and GDN_PROJ_ROWS % PROJ_HALO == 0
    assert conv_w.shape[1] % PROJ_SUB == 0 and (n_cols - conv_w.shape[1]) % PROJ_SUB == 0 and D_MODEL % PROJ_SUB == 0
    halo_blocks = GDN_PROJ_ROWS // PROJ_HALO
    return pl.pallas_call(
        functools.partial(_gdn_proj_kernel, seq=seq),
        grid=(rows // GDN_PROJ_ROWS,),
        in_specs=[
            pl.BlockSpec((GDN_PROJ_ROWS, D_MODEL), lambda i: (i, 0)),
            pl.BlockSpec((PROJ_HALO, D_MODEL), lambda i: (jnp.maximum(i * halo_blocks - 1, 0), 0)),
            _resident((1, D_MODEL)),
            _resident((D_MODEL, n_cols)),
            _resident((D_MODEL, LANE)),
            _resident(conv_w.shape),
        ],
        out_specs=[
            pl.BlockSpec((GDN_PROJ_ROWS, n_cols), lambda i: (i, 0)),
            pl.BlockSpec((GDN_PROJ_ROWS, LANE), lambda i: (i, 0)),
        ],
        out_shape=[
            jax.ShapeDtypeStruct((rows, n_cols), BF16),
            jax.ShapeDtypeStruct((rows, LANE), F32),
        ],
        scratch_shapes=[pltpu.VMEM((PROJ_HALO + GDN_PROJ_ROWS, D_MODEL), BF16),
                        pltpu.VMEM((2, PROJ_HALO + GDN_PROJ_ROWS, PROJ_SUB), F32)],
        compiler_params=_params(("parallel",)),
        name="gdn_proj",
    )(x2, x2, g, w, wsm, conv_w)


def _bias_tiles_kernel(tab_ref, diag_ref, corner_ref):
    h = pl.program_id(0)
    t = diag_ref.shape[1]
    cn = corner_ref.shape[1]

    def bias_of(rel):
        b = jnp.full(rel.shape, tab_ref[h, 0], F32)
        for bk in range(1, N_BUCKETS):
            b = jnp.where(rel >= BUCKET_LO[bk - 1], tab_ref[h, bk], b)
        return b

    rel = lax.broadcasted_iota(jnp.int32, (t, t), 1) - lax.broadcasted_iota(jnp.int32, (t, t), 0)
    diag_ref[0] = jnp.where(rel >= 0, bias_of(rel), NEG_INF)
    rel_c = (lax.broadcasted_iota(jnp.int32, (cn, cn), 1) - lax.broadcasted_iota(jnp.int32, (cn, cn), 0)) + cn
    corner_ref[0] = bias_of(rel_c) - tab_ref[h, N_BUCKETS - 1]


def _bias_tiles(rel_bias_t):
    t = ATT_BLOCK
    assert ATT_CORNER + 1 >= BUCKET_LO[-1], "entries outside the corner must sit in the last bucket"
    return pl.pallas_call(
        _bias_tiles_kernel,
        grid=(N_HEADS,),
        in_specs=[pl.BlockSpec(memory_space=pltpu.SMEM)],
        out_specs=[
            pl.BlockSpec((1, t, t), lambda h: (h, 0, 0)),
            pl.BlockSpec((1, ATT_CORNER, ATT_CORNER), lambda h: (h, 0, 0)),
        ],
        out_shape=[
            jax.ShapeDtypeStruct((N_HEADS, t, t), F32),
            jax.ShapeDtypeStruct((N_HEADS, ATT_CORNER, ATT_CORNER), F32),
        ],
        compiler_params=_params(("parallel",)),
        name="bias_tiles",
    )(rel_bias_t)


def _attn_kernel(qi_tab, kj_tab, q_ref, k_ref, v_ref, diag_ref, corner_ref, tab_ref, lam_ref, sg_ref, li_ref,
                 o_ref, qs_ref, vt_ref, m_ref, acc_ref):
    hg = pl.program_id(0)
    step = pl.program_id(2)
    qi = qi_tab[step]
    kj = kj_tab[step]
    t = q_ref.shape[1]
    cn = corner_ref.shape[1]
    heads = range(ATT_HEADS)
    cols = [slice(g * HEAD_W, (g + 1) * HEAD_W) for g in heads]

    @pl.when(kj == 0)
    def _():
        lane = lax.broadcasted_iota(jnp.int32, (t, HEAD_W), 1)
        for g in heads:
            q = q_ref[0, :, cols[g]]
            zero = jnp.zeros_like(q)
            qs_ref[g, 0] = jnp.where(lane < HEAD_DIM_A, q, zero)
            qs_ref[g, 1] = jnp.where(lane >= HEAD_DIM_A, q, zero)
            vt_ref[g, HEAD_W:HEAD_W + ATT_ONES, :] = jnp.ones((ATT_ONES, t), BF16)
        m_ref[...] = jnp.full(m_ref.shape, NEG_INF, F32)
        acc_ref[...] = jnp.zeros(acc_ref.shape, F32)

    for g in heads:
        vt_ref[g, 0:HEAD_W, :] = v_ref[0, :, cols[g]].astype(F32).T.astype(BF16)
    delta = qi - kj

    def sweep(bias_fn, c):
        units = [(g, j) for g in heads for j in range(2)]
        n = len(units)
        s, m_prev, m_new, p = {}, {}, {}, {}

        def logits(i):
            g, j = units[i]
            s[i] = bias_fn(_dot_nt(k_ref[0, :, cols[g]], qs_ref[g, j]), g)

        def softmax(i):
            g, j = units[i]
            m_prev[i] = m_ref[g, j]
            col_max = jnp.broadcast_to(jnp.max(s[i], axis=0, keepdims=True), (SUBLANE, t))
            m_new[i] = jnp.maximum(m_prev[i], col_max + c[g])
            p[i] = jnp.exp2(s.pop(i) - (m_new[i][0:1, :] - c[g])).astype(BF16)

        def values(i):
            g, j = units[i]
            pv = _dot(vt_ref[g], p.pop(i))
            alpha = jnp.exp2(m_prev.pop(i) - m_new[i])
            acc_ref[g, j] = alpha[0:1, :] * acc_ref[g, j] + pv
            m_ref[g, j] = m_new.pop(i)

        for i in range(n + 2 * ATT_SKEW):
            if i < n:
                logits(i)
            if 0 <= i - ATT_SKEW < n:
                softmax(i - ATT_SKEW)
            if 0 <= i - 2 * ATT_SKEW < n:
                values(i - 2 * ATT_SKEW)

    @pl.when(delta >= 1)
    def _():
        near = jnp.where(delta == 1, 1.0, 0.0)

        def bias_fn(s, g):
            bottom = jnp.concatenate([s[t - cn:t, 0:cn] + near * corner_ref[g], s[t - cn:t, cn:t]], axis=1)
            return jnp.concatenate([s[0:t - cn, :], bottom], axis=0)

        sweep(bias_fn, [tab_ref[hg * ATT_HEADS + g, N_BUCKETS - 1] for g in heads])

    @pl.when(delta == 0)
    def _():
        sweep(lambda s, g: s + diag_ref[g], [0.0] * ATT_HEADS)
        lq = lam_ref[...]
        lam_init = li_ref[0]
        lam = (jnp.exp(jnp.sum(lq[0:1] * lq[1:2], axis=-1, keepdims=True))
               - jnp.exp(jnp.sum(lq[2:3] * lq[3:4], axis=-1, keepdims=True)) + lam_init)
        for g in heads:
            a0 = acc_ref[g, 0]
            a1 = acc_ref[g, 1]
            o_t = (a0[0:HEAD_W] / a0[HEAD_W:HEAD_W + 1] - lam * (a1[0:HEAD_W] / a1[HEAD_W:HEAD_W + 1]))
            o_ref[0, :, cols[g]] = (_rmsnorm(o_t.T, sg_ref[...]) * (1.0 - lam_init)).astype(BF16)


def _attention(qkv, bias_diag, bias_corner, rel_bias_t, lam_a, subln, lam_init):
    bsz, seq, _ = qkv.shape
    t = ATT_BLOCK
    nq = seq // t
    assert ATT_CORNER <= t and N_HEADS % ATT_HEADS == 0
    n_groups = N_HEADS // ATT_HEADS
    gw = ATT_HEADS * HEAD_W
    pairs =[(qi, kj) for qi in range(nq) for kj in range(qi + 1)]
    qi_tab = jnp.asarray([p[0] for p in pairs], jnp.int32)
    kj_tab = jnp.asarray([p[1] for p in pairs], jnp.int32)
    grid_spec = pltpu.PrefetchScalarGridSpec(
        num_scalar_prefetch=2,
        grid=(n_groups, bsz, len(pairs)),
        in_specs=[
            pl.BlockSpec((1, t, gw), lambda h, b, s, qt, kt: (b, qt[s], h)),
            pl.BlockSpec((1, t, gw), lambda h, b, s, qt, kt: (b, kt[s], n_groups + h)),
            pl.BlockSpec((1, t, gw), lambda h, b, s, qt, kt: (b, kt[s], 2 * n_groups + h)),
            pl.BlockSpec((ATT_HEADS, t, t), lambda h, b, s, qt, kt: (h, 0, 0)),
            pl.BlockSpec((ATT_HEADS, ATT_CORNER, ATT_CORNER), lambda h, b, s, qt, kt: (h, 0, 0)),
            pl.BlockSpec(memory_space=pltpu.SMEM),
            pl.BlockSpec((4, HEAD_DIM_A), lambda h, b, s, qt, kt: (0, 0)),
            pl.BlockSpec((1, HEAD_W), lambda h, b, s, qt, kt: (0, 0)),
            pl.BlockSpec(memory_space=pltpu.SMEM),
        ],
        out_specs=pl.BlockSpec((1, t, gw), lambda h, b, s, qt, kt: (b, qt[s], h)),
        scratch_shapes=[
            pltpu.VMEM((ATT_HEADS, 2, t, HEAD_W), BF16),
            pltpu.VMEM((ATT_HEADS, HEAD_W + ATT_ONES, t), BF16),
            pltpu.VMEM((ATT_HEADS, 2, SUBLANE, t), F32),
            pltpu.VMEM((ATT_HEADS, 2, HEAD_W + ATT_ONES, t), F32),
        ],
    )
    return pl.pallas_call(
        _attn_kernel,
        grid_spec=grid_spec,
        out_shape=jax.ShapeDtypeStruct((bsz, seq, D_MODEL), BF16),
        compiler_params=_params(("parallel", "parallel", "arbitrary")),
        name="diff_attention",
    )(qi_tab, kj_tab, qkv, qkv, qkv, bias_diag, bias_corner, rel_bias_t, lam_a, subln, lam_init)


def _split(x, n_pieces):
    pieces = []
    for _ in range(n_pieces):
        piece = x.astype(BF16)
        pieces.append(piece)
        x = x - piece.astype(F32)
    return pieces


def _gdn_kernel(q_ref, k_ref, v_ref, z_ref, sm_ref, alog_ref, dtb_ref, on_ref, tril_ref, lvl_ref,
                o_ref, state_ref):
    c = GDN_CHUNK

    @pl.when(pl.program_id(1) == 0)
    def _():
        state_ref[...] = jnp.zeros(state_ref.shape, F32)

    sm = sm_ref[0]
    beta_all = _sigmoid(sm)
    xa = sm + dtb_ref[...]
    softplus = jnp.maximum(xa, 0.0) + jnp.log1p(jnp.exp(-jnp.abs(xa)))
    g_all = -jnp.exp(alog_ref[...]) * softplus
    gc_all = _dot(tril_ref[...], jnp.concatenate(_split(g_all, 3), axis=0))
    eg_all = jnp.exp(gc_all)
    ed_all = jnp.exp(gc_all[c - 1:c, :] - gc_all)

    ii = lax.broadcasted_iota(jnp.int32, (c, c), 0)
    jj = lax.broadcasted_iota(jnp.int32, (c, c), 1)
    tril = ii >= jj
    strict = ii > jj
    n_levels = lvl_ref.shape[0] - 1

    heads = range(N_HEADS)
    cols = [slice(h * HEAD_W, (h + 1) * HEAD_W) for h in heads]
    beta = [beta_all[:, h:h + 1] for h in heads]
    gcr = [jnp.broadcast_to(gc_all[:, N_HEADS + h:N_HEADS + h + 1], (c, c)) for h in heads]
    eg = [eg_all[:, N_HEADS + h:N_HEADS + h + 1] for h in heads]
    ed = [ed_all[:, N_HEADS + h:N_HEADS + h + 1] for h in heads]
    k = [k_ref[0, :, cols[h]] for h in heads]
    kq = [_dot_nt(jnp.concatenate([k[h], q_ref[0, :, cols[h]]], axis=0), k[h]) for h in heads]
    decay = [jnp.where(tril, jnp.exp(gcr[h] - gcr[h].T), 0.0) for h in heads]
    a_b = [jnp.where(strict, kq[h][0:c] * (decay[h] * beta[h]), 0.0).astype(BF16) for h in heads]
    attn_b = [(kq[h][c:2 * c] * decay[h]).astype(BF16) for h in heads]
    inv = [lvl_ref[0] - a_b[h] * lvl_ref[1] for h in heads]
    for lv in range(2, n_levels + 1):
        x = [_dot(a_b[h] * lvl_ref[lv], inv[h]).astype(BF16) for h in heads]
        inv = [inv[h] - _dot(inv[h], x[h]).astype(BF16) for h in heads]
    rhs = [jnp.concatenate([k[h] * (beta[h] * eg[h]).astype(BF16),
                            v_ref[0, :, cols[h]] * beta[h].astype(BF16)], axis=1) for h in heads]
    wu = [_dot(inv[h], rhs[h]).astype(BF16) for h in heads]
    aw = [_dot(attn_b[h], wu[h]) for h in heads]
    pn = [_dot_tn(k[h] * ed[h].astype(BF16), wu[h]) for h in heads]
    q_eff = [q_ref[0, :, cols[h]].astype(F32) * eg[h] - aw[h][:, 0:HEAD_W] for h in heads]
    state = [state_ref[h] for h in heads]
    r = [_dot(jnp.concatenate([pn[h][:, 0:HEAD_W], q_eff[h]], axis=0).astype(BF16), state[h].astype(BF16))
         for h in heads]
    for h in heads:
        state_ref[h] = state[h] * eg[h][c - 1:c, :] + pn[h][:, HEAD_W:2 * HEAD_W] - r[h][0:DK_B]
    for h in heads:
        o = r[h][DK_B:DK_B + c] + aw[h][:, HEAD_W:2 * HEAD_W]
        z = z_ref[0, :, cols[h]].astype(F32)
        o_ref[0, :, cols[h]] = (_rmsnorm(o, on_ref[...]) * _silu(z)).astype(BF16)


def _gdn_constants():
    c = GDN_CHUNK
    r = np.arange(c)
    tril = (r[None, :] <= r[:, None]).astype(np.float32)
    tril3 = np.concatenate([tril, tril, tril], axis=1)
    levels = [np.eye(c, dtype=np.float32)]
    s = 1
    while s < c:
        same = (r[:, None] ^ r[None, :]) < 2 * s
        levels.append((same & ((r[:, None] & s) != 0) & ((r[None, :] & s) == 0)).astype(np.float32))
        s *= 2
    return jnp.asarray(tril3, BF16), jnp.asarray(np.stack(levels), BF16)


def _gdn(proj_b, small, alog_row, dtb_row, onorm):
    bsz, seq, _ = proj_b.shape
    c = GDN_CHUNK
    assert c == LANE and seq % c == 0
    tril3, levels = _gdn_constants()
    blk = lambda col: pl.BlockSpec((1, c, D_MODEL), lambda b, s: (b, s, col))
    return pl.pallas_call(
        _gdn_kernel,
        grid=(bsz, seq // c),
        in_specs=[
            blk(0), blk(1), blk(2), blk(3),
            pl.BlockSpec((1, c, LANE), lambda b, s: (b, s, 0)),
            pl.BlockSpec((1, LANE), lambda b, s: (0, 0)),
            pl.BlockSpec((1, LANE), lambda b, s: (0, 0)),
            pl.BlockSpec((1, HEAD_W), lambda b, s: (0, 0)),
            _resident(tril3.shape), _resident(levels.shape),
        ],
        out_specs=pl.BlockSpec((1, c, D_MODEL), lambda b, s: (b, s, 0)),
        out_shape=jax.ShapeDtypeStruct((bsz, seq, D_MODEL), BF16),
        scratch_shapes=[pltpu.VMEM((N_HEADS, DK_B, HEAD_W), F32)],
        compiler_params=_params(("parallel", "arbitrary")),
        name="gated_delta_rule",
    )(proj_b, proj_b, proj_b, proj_b, small, alog_row, dtb_row, onorm, tril3, levels)


def _merge_kernel(x_ref, oa_ref, ob_ref, ga_ref, gb_ref, wa_ref, wb_ref, wo_ref, o_ref):
    ya = _dot(oa_ref[...], wa_ref[...])
    yb = _dot(ob_ref[...], wb_ref[...])
    mixed = _sigmoid(ga_ref[...].astype(F32)) * ya + _sigmoid(gb_ref[...].astype(F32)) * yb
    o_ref[...] = x_ref[...] + _dot(mixed.astype(BF16), wo_ref[...])


def _merge(x2, oa, ob, proj_b, wa, wb, wo):
    rows = x2.shape[0]
    gate_block = 4
    row_spec = pl.BlockSpec((MERGE_ROWS, D_MODEL), lambda i: (i, 0))
    return pl.pallas_call(
        _merge_kernel,
        grid=(rows // MERGE_ROWS,),
        in_specs=[
            row_spec, row_spec, row_spec,
            pl.BlockSpec((MERGE_ROWS, D_MODEL), lambda i: (i, gate_block)),
            pl.BlockSpec((MERGE_ROWS, D_MODEL), lambda i: (i, gate_block + 1)),
            _resident((D_MODEL, D_MODEL)), _resident((D_MODEL, D_MODEL)), _resident((D_MODEL, D_MODEL)),
        ],
        out_specs=row_spec,
        out_shape=jax.ShapeDtypeStruct((rows, D_MODEL), F32),
        compiler_params=_params(("parallel",)),
        name="branch_merge",
    )(x2, oa, ob, proj_b, proj_b, wa, wb, wo)


def _ffn_ple_kernel(x_ref, halo_ref, p_ref, fg_ref, wup_ref, cw_ref, cb_ref, wdn_ref, pg_ref, wg_ref, wp_ref,
                    o_ref, act_ref, *, seq):
    rows = x_ref.shape[0]
    x = x_ref[...]
    fg = fg_ref[...]
    keep = jnp.where((pl.program_id(0) * rows) % seq == 0, 0.0, 1.0)
    h_ext = jnp.concatenate([_rmsnorm(halo_ref[...], fg) * keep, _rmsnorm(x, fg)], axis=0).astype(BF16)
    off = HALO - (CONV_FF - 1)

    def conv(u, lo):
        last = CONV_FF - 1
        y = cb_ref[:, lo:lo + FFN_COLS] + cw_ref[last:CONV_FF, lo:lo + FFN_COLS] * u[off + last:off + last + rows]
        for tap in range(last):
            y = y + cw_ref[tap:tap + 1, lo:lo + FFN_COLS] * u[off + tap:off + tap + rows]
        return y

    for c in range(D_FF // FFN_COLS):
        lo = c * FFN_COLS
        yg = conv(_dot(h_ext, wup_ref[:, lo:lo + FFN_COLS]), lo)
        yv = conv(_dot(h_ext, wup_ref[:, D_FF + lo:D_FF + lo + FFN_COLS]), D_FF + lo)
        act_ref[:, lo:lo + FFN_COLS] = (_silu(yg) * yv).astype(BF16)

    x1 = x + _dot(act_ref[...], wdn_ref[...])
    h2 = _rmsnorm(x1, pg_ref[...]).astype(BF16)
    gate = _sigmoid(_dot(h2, wg_ref[...]))
    o_ref[...] = x1 + gate * _dot(p_ref[...].astype(BF16), wp_ref[...])


def _ffn_ple(x2, p2, fg, wup, cw, cb, wdn, pg, wg, wp, seq):
    rows = x2.shape[0]
    assert seq % FFN_ROWS == 0 and D_FF % FFN_COLS == 0
    halo_blocks = FFN_ROWS // HALO
    return pl.pallas_call(
        functools.partial(_ffn_ple_kernel, seq=seq),
        grid=(rows // FFN_ROWS,),
        in_specs=[
            pl.BlockSpec((FFN_ROWS, D_MODEL), lambda i: (i, 0)),
            pl.BlockSpec((HALO, D_MODEL), lambda i: (jnp.maximum(i * halo_blocks - 1, 0), 0)),
            pl.BlockSpec((FFN_ROWS, D_PLE), lambda i: (i, 0)),
            _resident((1, D_MODEL)),
            _resident((D_MODEL, 2 * D_FF)),
            _resident((CONV_FF, 2 * D_FF)),
            _resident((1, 2 * D_FF)),
            _resident((D_FF, D_MODEL)),
            _resident((1, D_MODEL)),
            _resident((D_MODEL, D_MODEL)),
            _resident((D_PLE, D_MODEL)),
        ],
        out_specs=pl.BlockSpec((FFN_ROWS, D_MODEL), lambda i: (i, 0)),
        out_shape=jax.ShapeDtypeStruct((rows, D_MODEL), F32),
        scratch_shapes=[pltpu.VMEM((FFN_ROWS, D_FF), BF16)],
        compiler_params=_params(("arbitrary",)),
        name="conv_ffn_ple",
    )(x2, x2, p2, fg, wup, cw, cb, wdn, pg, wg, wp)


def kernel(x, p, w_in, attn_norm, qnorm_a, knorm_a, lam_a, subln_a, rel_bias, conv_b, a_log_b, dt_bias_b, onorm_b, w_br_a, w_br_b, w_out, ffn_norm, w_up, conv_ff, conv_ff_bias, w_down, ple_norm, w_ple_gate, w_ple_proj):
    bsz, seq, _ = x.shape
    depth = w_in.shape[0]
    rows = bsz * seq
    assert rows % PROJ_ROWS == 0 and seq % ATT_BLOCK == 0 and rows % MERGE_ROWS == 0

    n_a = 3 * D_MODEL
    n_b = 4 * D_MODEL
    w_in_b = w_in.astype(BF16)
    rel_bias_t = rel_bias.astype(F32).T * LOG2E
    bias_diag, bias_corner = _bias_tiles(rel_bias_t)
    n_maps = D_MODEL // HEAD_DIM_A

    x2 = x.reshape(rows, D_MODEL).astype(F32)
    for i in range(depth):
        lam_init = 0.8 - 0.6 * math.exp(-0.3 * i)
        w_a = w_in_b[i, :, 0:n_a]
        w_b = jnp.concatenate([w_in_b[i, :, n_a:n_a + n_b], w_in_b[i, :, n_a + n_b + 2 * N_HEADS:]], axis=1)
        w_small = jnp.pad(w_in_b[i, :, n_a + n_b:n_a + n_b + 2 * N_HEADS], ((0, 0), (0, LANE - 2 * N_HEADS)))
        gains = jnp.stack([
            jnp.tile(qnorm_a[i].astype(F32), n_maps) * (HEAD_DIM_A ** -0.5 * LOG2E),
            jnp.tile(knorm_a[i].astype(F32), n_maps),
            jnp.ones((D_MODEL,), F32),
        ]).reshape(3, 1, D_MODEL)
        g_attn = attn_norm[i].astype(F32).reshape(1, D_MODEL)

        qkv = _attn_proj(x2, g_attn, w_a, gains)
        proj_b, small = _gdn_proj(x2, g_attn, w_b, w_small, conv_b[i].astype(F32), seq)

        oa = _attention(qkv.reshape(bsz, seq, n_a), bias_diag, bias_corner, rel_bias_t, lam_a[i].astype(F32),
                        subln_a[i].astype(F32).reshape(1, HEAD_W), jnp.full((1,), lam_init, F32))
        head_lanes = (N_HEADS, LANE - 2 * N_HEADS)
        ob = _gdn(proj_b.reshape(bsz, seq, -1), small.reshape(bsz, seq, LANE),
                  jnp.pad(a_log_b[i].astype(F32), head_lanes).reshape(1, LANE),
                  jnp.pad(dt_bias_b[i].astype(F32), head_lanes).reshape(1, LANE),
                  onorm_b[i].astype(F32).reshape(1, HEAD_W))

        x2 = _merge(x2, oa.reshape(rows, D_MODEL), ob.reshape(rows, D_MODEL), proj_b,
                    w_br_a[i].astype(BF16), w_br_b[i].astype(BF16), w_out[i].astype(BF16))

        x2 = _ffn_ple(x2, p[i].reshape(rows, D_PLE), ffn_norm[i].astype(F32).reshape(1, D_MODEL),
                      w_up[i].astype(BF16), conv_ff[i].astype(F32), conv_ff_bias[i].astype(F32).reshape(1, 2 * D_FF),
                      w_down[i].astype(BF16), ple_norm[i].astype(F32).reshape(1, D_MODEL),
                      w_ple_gate[i].astype(BF16), w_ple_proj[i].astype(BF16), seq)
    return x2.reshape(bsz, seq, D_MODEL).astype(x.dtype)
```

```python
import functools
import math

import numpy as np
import jax
import jax.numpy as jnp
from jax import lax
from jax.experimental import pallas as pl
from jax.experimental.pallas import tpu as pltpu

F32 = jnp.float32
BF16 = jnp.bfloat16

D_MODEL = 1024
N_HEADS = 8
HEAD_W = 128
HEAD_DIM_A = 64
DK_B = 128
CONV_B = 4
CHUNK_B = 64
D_FF = 2816
CONV_FF = 3
D_PLE = 256
N_BUCKETS = 32
MAX_DISTANCE = 128
EPS = 1e-6
NEG_INF = -1e30

LANE = 128
SUBLANE = 8
VMEM_LIMIT_BYTES = 56 * 1024 * 1024

PROJ_ROWS = 1024
PROJ_COLS = 1024
PROJ_SUB = 256
GDN_PROJ_ROWS = 512
ATT_BLOCK = 512
ATT_HEADS = 8
ATT_ONES = 2 * SUBLANE
ATT_SKEW = 1
ATT_CORNER = 128
LOG2E = math.log2(math.e)
GDN_CHUNK = 128
GDN_STEP_CHUNKS = 2
MERGE_ROWS = 512
FFN_ROWS = 512
FFN_COLS = 256
HALO = SUBLANE
PROJ_HALO = 2 * SUBLANE


def _bucket_thresholds():
    n = np.arange(0, 4 * MAX_DISTANCE)
    max_exact = N_BUCKETS // 2
    nf = np.maximum(n, 1).astype(np.float32)
    large = max_exact + (np.log(nf / np.float32(max_exact)) / np.float32(math.log(MAX_DISTANCE / max_exact))
                         * np.float32(N_BUCKETS - max_exact)).astype(np.int32)
    bucket = np.where(n < max_exact, n, np.minimum(large, N_BUCKETS - 1))
    return [int(np.argmax(bucket >= b)) for b in range(1, N_BUCKETS)]


BUCKET_LO = _bucket_thresholds()


def _params(semantics):
    return pltpu.CompilerParams(dimension_semantics=semantics, vmem_limit_bytes=VMEM_LIMIT_BYTES)


def _resident(shape):
    zeros = (0,) * len(shape)
    return pl.BlockSpec(shape, lambda *_: zeros, pipeline_mode=pl.Buffered(1))


def _rmsnorm(x, g):
    ms = jnp.mean(x * x, axis=-1, keepdims=True)
    return x * lax.rsqrt(ms + EPS) * g


def _sigmoid(x):
    return 1.0 / (1.0 + jnp.exp2(x * (-LOG2E)))


def _silu(x):
    return x * _sigmoid(x)


def _dot(a, b):
    return jnp.dot(a, b, preferred_element_type=F32)


def _dot_nt(a, b):
    return lax.dot_general(a, b, (((1,), (1,)), ((), ())), preferred_element_type=F32)


def _dot_tn(a, b):
    return lax.dot_general(a, b, (((0,), (0,)), ((), ())), preferred_element_type=F32)


def _attn_proj_kernel(x_ref, g_ref, w_ref, gain_ref, gsum_ref, o_ref, h_ref):
    j = pl.program_id(1)

    @pl.when(j == 0)
    def _():
        h_ref[...] = _rmsnorm(x_ref[...], g_ref[...]).astype(BF16)

    @pl.when(j < 2)
    def _():
        n_sub = PROJ_COLS // PROJ_SUB

        def project(sub):
            return _dot(h_ref[...], w_ref[:, sub * PROJ_SUB:(sub + 1) * PROJ_SUB])

        def normalise(sub, y):
            cols = slice(sub * PROJ_SUB, (sub + 1) * PROJ_SUB)
            ss = _dot((y * y).astype(BF16), gsum_ref[...])
            r = lax.rsqrt(ss * (1.0 / HEAD_DIM_A) + EPS)
            o_ref[:, cols] = (y * r * gain_ref[:, cols]).astype(BF16)

        y = project(0)
        for sub in range(n_sub):
            y_next = project(sub + 1) if sub + 1 < n_sub else None
            normalise(sub, y)
            y = y_next

    @pl.when(j == 2)
    def _():
        o_ref[...] = _dot(h_ref[...], w_ref[...]).astype(BF16)


def _attn_proj(x2, g, w, gains):
    rows = x2.shape[0]
    assert PROJ_SUB % HEAD_DIM_A == 0
    group = np.arange(PROJ_SUB) // HEAD_DIM_A
    gsum = (group[:, None] == group[None, :]).astype(np.float32)
    return pl.pallas_call(
        _attn_proj_kernel,
        grid=(rows // PROJ_ROWS, 3),
        in_specs=[
            pl.BlockSpec((PROJ_ROWS, D_MODEL), lambda i, j: (i, 0)),
            pl.BlockSpec((1, D_MODEL), lambda i, j: (0, 0)),
            pl.BlockSpec((D_MODEL, PROJ_COLS), lambda i, j: (0, j)),
            pl.BlockSpec((None, 1, PROJ_COLS), lambda i, j: (j, 0, 0)),
            pl.BlockSpec((PROJ_SUB, PROJ_SUB), lambda i, j: (0, 0)),
        ],
        out_specs=pl.BlockSpec((PROJ_ROWS, PROJ_COLS), lambda i, j: (i, j)),
        out_shape=jax.ShapeDtypeStruct((rows, 3 * D_MODEL), BF16),
        scratch_shapes=[pltpu.VMEM((PROJ_ROWS, D_MODEL), BF16)],
        compiler_params=_params(("parallel", "arbitrary")),
        name="attn_proj",
    )(x2, g, w, gains, jnp.asarray(gsum, BF16))


def _gdn_proj_kernel(x_ref, halo_ref, g_ref, w_ref, wsm_ref, cw_ref, o_ref, osm_ref, h_ref, u_ref, *, seq):
    rows = x_ref.shape[0]
    g = g_ref[...]
    keep = jnp.where((pl.program_id(0) * rows) % seq == 0, 0.0, 1.0)
    h_ref[0:PROJ_HALO, :] = (_rmsnorm(halo_ref[...], g) * keep).astype(BF16)
    h = _rmsnorm(x_ref[...], g).astype(BF16)
    h_ref[PROJ_HALO:, :] = h
    osm_ref[...] = _dot(h, wsm_ref[...])

    n_conv_cols = cw_ref.shape[1]
    n_conv = n_conv_cols // PROJ_SUB
    n_plain = (w_ref.shape[1] - n_conv_cols) // PROJ_SUB
    off = PROJ_HALO - (CONV_B - 1)

    def project_conv(sub):
        u_ref[sub % 2] = _dot(h_ref[...], w_ref[:, sub * PROJ_SUB:(sub + 1) * PROJ_SUB])

    def project_plain(sub):
        cols = slice(n_conv_cols + sub * PROJ_SUB, n_conv_cols + (sub + 1) * PROJ_SUB)
        o_ref[:, cols] = _dot(h_ref[PROJ_HALO:, :], w_ref[:, cols]).astype(BF16)

    def epilogue(sub):
        kind = sub * PROJ_SUB // D_MODEL
        u = u_ref.at[sub % 2]
        for hh in range(PROJ_SUB // HEAD_W):
            lo = hh * HEAD_W
            col = sub * PROJ_SUB + lo
            last = CONV_B - 1
            y = cw_ref[last:CONV_B, col:col + HEAD_W] * u[off + last:off + last + rows, lo:lo + HEAD_W]
            for tap in range(last):
                y = y + cw_ref[tap:tap + 1, col:col + HEAD_W] * u[off + tap:off + tap + rows, lo:lo + HEAD_W]
            y = _silu(y)
            if kind < 2:
                y = y * (lax.rsqrt(jnp.sum(y * y, axis=-1, keepdims=True) + EPS) * (DK_B ** -0.5 if kind == 0 else 1.0))
            o_ref[:, col:col + HEAD_W] = y.astype(BF16)

    project_conv(0)
    for sub in range(max(n_conv, n_plain)):
        if sub + 1 < n_conv:
            project_conv(sub + 1)
        if sub < n_plain:
            project_plain(sub)
        if sub < n_conv:
            epilogue(sub)


def _gdn_proj(x2, g, w, wsm, conv_w, seq):
    rows = x2.shape[0]
    n_cols = w.shape[1]
    assert seq % GDN_PROJ_ROWS == 0 and GDN_PROJ_ROWS % PROJ_HALO == 0
    assert conv_w.shape[1] % PROJ_SUB == 0 and (n_cols - conv_w.shape[1]) % PROJ_SUB == 0 and D_MODEL % PROJ_SUB == 0
    halo_blocks = GDN_PROJ_ROWS // PROJ_HALO
    return pl.pallas_call(
        functools.partial(_gdn_proj_kernel, seq=seq),
        grid=(rows // GDN_PROJ_ROWS,),
        in_specs=[
            pl.BlockSpec((GDN_PROJ_ROWS, D_MODEL), lambda i: (i, 0)),
            pl.BlockSpec((PROJ_HALO, D_MODEL), lambda i: (jnp.maximum(i * halo_blocks - 1, 0), 0)),
            _resident((1, D_MODEL)),
            _resident((D_MODEL, n_cols)),
            _resident((D_MODEL, LANE)),
            _resident(conv_w.shape),
        ],
        out_specs=[
            pl.BlockSpec((GDN_PROJ_ROWS, n_cols), lambda i: (i, 0)),
            pl.BlockSpec((GDN_PROJ_ROWS, LANE), lambda i: (i, 0)),
        ],
        out_shape=[
            jax.ShapeDtypeStruct((rows, n_cols), BF16),
            jax.ShapeDtypeStruct((rows, LANE), F32),
        ],
        scratch_shapes=[pltpu.VMEM((PROJ_HALO + GDN_PROJ_ROWS, D_MODEL), BF16),
                        pltpu.VMEM((2, PROJ_HALO + GDN_PROJ_ROWS, PROJ_SUB), F32)],
        compiler_params=_params(("parallel",)),
        name="gdn_proj",
    )(x2, x2, g, w, wsm, conv_w)


def _bias_tiles_kernel(tab_ref, diag_ref, corner_ref):
    h = pl.program_id(0)
    t = diag_ref.shape[1]
    cn = corner_ref.shape[1]

    def bias_of(rel):
        b = jnp.full(rel.shape, tab_ref[h, 0], F32)
        for bk in range(1, N_BUCKETS):
            b = jnp.where(rel >= BUCKET_LO[bk - 1], tab_ref[h, bk], b)
        return b

    rel = lax.broadcasted_iota(jnp.int32, (t, t), 1) - lax.broadcasted_iota(jnp.int32, (t, t), 0)
    diag_ref[0] = jnp.where(rel >= 0, bias_of(rel), NEG_INF)
    rel_c = (lax.broadcasted_iota(jnp.int32, (cn, cn), 1) - lax.broadcasted_iota(jnp.int32, (cn, cn), 0)) + cn
    corner_ref[0] = bias_of(rel_c) - tab_ref[h, N_BUCKETS - 1]


def _bias_tiles(rel_bias_t):
    t = ATT_BLOCK
    assert ATT_CORNER + 1 >= BUCKET_LO[-1], "entries outside the corner must sit in the last bucket"
    return pl.pallas_call(
        _bias_tiles_kernel,
        grid=(N_HEADS,),
        in_specs=[pl.BlockSpec(memory_space=pltpu.SMEM)],
        out_specs=[
            pl.BlockSpec((1, t, t), lambda h: (h, 0, 0)),
            pl.BlockSpec((1, ATT_CORNER, ATT_CORNER), lambda h: (h, 0, 0)),
        ],
        out_shape=[
            jax.ShapeDtypeStruct((N_HEADS, t, t), F32),
            jax.ShapeDtypeStruct((N_HEADS, ATT_CORNER, ATT_CORNER), F32),
        ],
        compiler_params=_params(("parallel",)),
        name="bias_tiles",
    )(rel_bias_t)


def _attn_kernel(qi_tab, kj_tab, q_ref, k_ref, v_ref, diag_ref, corner_ref, tab_ref, lam_ref, sg_ref, li_ref,
                 o_ref, qs_ref, vt_ref, m_ref, acc_ref):
    hg = pl.program_id(0)
    step = pl.program_id(2)
    qi = qi_tab[step]
    kj = kj_tab[step]
    t = q_ref.shape[1]
    cn = corner_ref.shape[1]
    heads = range(ATT_HEADS)
    cols = [slice(g * HEAD_W, (g + 1) * HEAD_W) for g in heads]

    @pl.when(kj == 0)
    def _():
        lane = lax.broadcasted_iota(jnp.int32, (t, HEAD_W), 1)
        for g in heads:
            q = q_ref[0, :, cols[g]]
            zero = jnp.zeros_like(q)
            qs_ref[g, 0] = jnp.where(lane < HEAD_DIM_A, q, zero)
            qs_ref[g, 1] = jnp.where(lane >= HEAD_DIM_A, q, zero)
            vt_ref[g, HEAD_W:HEAD_W + ATT_ONES, :] = jnp.ones((ATT_ONES, t), BF16)
        m_ref[...] = jnp.full(m_ref.shape, NEG_INF, F32)
        acc_ref[...] = jnp.zeros(acc_ref.shape, F32)

    for g in heads:
        vt_ref[g, 0:HEAD_W, :] = v_ref[0, :, cols[g]].astype(F32).T.astype(BF16)
    delta = qi - kj

    def sweep(bias_fn, c):
        units = [(g, j) for g in heads for j in range(2)]
        n = len(units)
        s, m_prev, m_new, p = {}, {}, {}, {}

        def logits(i):
            g, j = units[i]
            s[i] = bias_fn(_dot_nt(k_ref[0, :, cols[g]], qs_ref[g, j]), g)

        def softmax(i):
            g, j = units[i]
            m_prev[i] = m_ref[g, j]
            col_max = jnp.broadcast_to(jnp.max(s[i], axis=0, keepdims=True), (SUBLANE, t))
            m_new[i] = jnp.maximum(m_prev[i], col_max + c[g])
            p[i] = jnp.exp2(s.pop(i) - (m_new[i][0:1, :] - c[g])).astype(BF16)

        def values(i):
            g, j = units[i]
            pv = _dot(vt_ref[g], p.pop(i))
            alpha = jnp.exp2(m_prev.pop(i) - m_new[i])
            acc_ref[g, j] = alpha[0:1, :] * acc_ref[g, j] + pv
            m_ref[g, j] = m_new.pop(i)

        for i in range(n + 2 * ATT_SKEW):
            if i < n:
                logits(i)
            if 0 <= i - ATT_SKEW < n:
                softmax(i - ATT_SKEW)
            if 0 <= i - 2 * ATT_SKEW < n:
                values(i - 2 * ATT_SKEW)

    @pl.when(delta >= 1)
    def _():
        near = jnp.where(delta == 1, 1.0, 0.0)

        def bias_fn(s, g):
            bottom = jnp.concatenate([s[t - cn:t, 0:cn] + near * corner_ref[g], s[t - cn:t, cn:t]], axis=1)
            return jnp.concatenate([s[0:t - cn, :], bottom], axis=0)

        sweep(bias_fn, [tab_ref[hg * ATT_HEADS + g, N_BUCKETS - 1] for g in heads])

    @pl.when(delta == 0)
    def _():
        sweep(lambda s, g: s + diag_ref[g], [0.0] * ATT_HEADS)
        lq = lam_ref[...]
        lam_init = li_ref[0]
        lam = (jnp.exp(jnp.sum(lq[0:1] * lq[1:2], axis=-1, keepdims=True))
               - jnp.exp(jnp.sum(lq[2:3] * lq[3:4], axis=-1, keepdims=True)) + lam_init)
        for g in heads:
            a0 = acc_ref[g, 0]
            a1 = acc_ref[g, 1]
            o_t = (a0[0:HEAD_W] / a0[HEAD_W:HEAD_W + 1] - lam * (a1[0:HEAD_W] / a1[HEAD_W:HEAD_W + 1]))
            o_ref[0, :, cols[g]] = (_rmsnorm(o_t.T, sg_ref[...]) * (1.0 - lam_init)).astype(BF16)


def _attention(qkv, bias_diag, bias_corner, rel_bias_t, lam_a, subln, lam_init):
    bsz, seq, _ = qkv.shape
    t = ATT_BLOCK
    nq = seq // t
    assert ATT_CORNER <= t and N_HEADS % ATT_HEADS == 0
    n_groups = N_HEADS // ATT_HEADS
    gw = ATT_HEADS * HEAD_W
    pairs =[(qi, kj) for qi in range(nq) for kj in range(qi + 1)]
    qi_tab = jnp.asarray([p[0] for p in pairs], jnp.int32)
    kj_tab = jnp.asarray([p[1] for p in pairs], jnp.int32)
    grid_spec = pltpu.PrefetchScalarGridSpec(
        num_scalar_prefetch=2,
        grid=(n_groups, bsz, len(pairs)),
        in_specs=[
            pl.BlockSpec((1, t, gw), lambda h, b, s, qt, kt: (b, qt[s], h)),
            pl.BlockSpec((1, t, gw), lambda h, b, s, qt, kt: (b, kt[s], n_groups + h)),
            pl.BlockSpec((1, t, gw), lambda h, b, s, qt, kt: (b, kt[s], 2 * n_groups + h)),
            pl.BlockSpec((ATT_HEADS, t, t), lambda h, b, s, qt, kt: (h, 0, 0)),
            pl.BlockSpec((ATT_HEADS, ATT_CORNER, ATT_CORNER), lambda h, b, s, qt, kt: (h, 0, 0)),
            pl.BlockSpec(memory_space=pltpu.SMEM),
            pl.BlockSpec((4, HEAD_DIM_A), lambda h, b, s, qt, kt: (0, 0)),
            pl.BlockSpec((1, HEAD_W), lambda h, b, s, qt, kt: (0, 0)),
            pl.BlockSpec(memory_space=pltpu.SMEM),
        ],
        out_specs=pl.BlockSpec((1, t, gw), lambda h, b, s, qt, kt: (b, qt[s], h)),
        scratch_shapes=[
            pltpu.VMEM((ATT_HEADS, 2, t, HEAD_W), BF16),
            pltpu.VMEM((ATT_HEADS, HEAD_W + ATT_ONES, t), BF16),
            pltpu.VMEM((ATT_HEADS, 2, SUBLANE, t), F32),
            pltpu.VMEM((ATT_HEADS, 2, HEAD_W + ATT_ONES, t), F32),
        ],
    )
    return pl.pallas_call(
        _attn_kernel,
        grid_spec=grid_spec,
        out_shape=jax.ShapeDtypeStruct((bsz, seq, D_MODEL), BF16),
        compiler_params=_params(("parallel", "parallel", "arbitrary")),
        name="diff_attention",
    )(qi_tab, kj_tab, qkv, qkv, qkv, bias_diag, bias_corner, rel_bias_t, lam_a, subln, lam_init)


def _split(x, n_pieces):
    pieces = []
    for _ in range(n_pieces):
        piece = x.astype(BF16)
        pieces.append(piece)
        x = x - piece.astype(F32)
    return pieces


def _gdn_kernel(q_ref, k_ref, v_ref, z_ref, sm_ref, alog_ref, dtb_ref, on_ref, tril_ref, lvl_ref,
                o_ref, state_ref):
    c = GDN_CHUNK

    @pl.when(pl.program_id(1) == 0)
    def _():
        state_ref[...] = jnp.zeros(state_ref.shape, F32)

    chunks = range(q_ref.shape[1] // c)
    rows = [slice(ci * c, (ci + 1) * c) for ci in chunks]

    beta_all, gc_all, eg_all, ed_all = [], [], [], []
    for ci in chunks:
        sm = sm_ref[0, rows[ci], :]
        xa = sm + dtb_ref[...]
        softplus = jnp.maximum(xa, 0.0) + jnp.log1p(jnp.exp(-jnp.abs(xa)))
        g_all = -jnp.exp(alog_ref[...]) * softplus
        gc = _dot(tril_ref[...], jnp.concatenate(_split(g_all, 3), axis=0))
        beta_all.append(_sigmoid(sm))
        gc_all.append(gc)
        eg_all.append(jnp.exp(gc))
        ed_all.append(jnp.exp(gc[c - 1:c, :] - gc))

    ii = lax.broadcasted_iota(jnp.int32, (c, c), 0)
    jj = lax.broadcasted_iota(jnp.int32, (c, c), 1)
    tril = ii >= jj
    strict = ii > jj
    n_levels = lvl_ref.shape[0] - 1

    heads = range(N_HEADS)
    cols = [slice(h * HEAD_W, (h + 1) * HEAD_W) for h in heads]
    units = [(ci, h) for ci in chunks for h in heads]
    idx = range(len(units))
    lane_b = [slice(h, h + 1) for _, h in units]
    lane_g = [slice(N_HEADS + h, N_HEADS + h + 1) for _, h in units]
    at = [(0, rows[ci], cols[h]) for ci, h in units]
    beta = [beta_all[units[u][0]][:, lane_b[u]] for u in idx]
    gcr = [jnp.broadcast_to(gc_all[units[u][0]][:, lane_g[u]], (c, c)) for u in idx]
    eg = [eg_all[units[u][0]][:, lane_g[u]] for u in idx]
    ed = [ed_all[units[u][0]][:, lane_g[u]] for u in idx]
    k = [k_ref[at[u]] for u in idx]
    kq = [_dot_nt(jnp.concatenate([k[u], q_ref[at[u]]], axis=0), k[u]) for u in idx]
    decay = [jnp.where(tril, jnp.exp(gcr[u] - gcr[u].T), 0.0) for u in idx]
    a_b = [jnp.where(strict, kq[u][0:c] * (decay[u] * beta[u]), 0.0).astype(BF16) for u in idx]
    attn_b = [(kq[u][c:2 * c] * decay[u]).astype(BF16) for u in idx]
    inv = [lvl_ref[0] - a_b[u] * lvl_ref[1] for u in idx]
    for lv in range(2, n_levels + 1):
        x = [_dot(a_b[u] * lvl_ref[lv], inv[u]).astype(BF16) for u in idx]
        inv = [inv[u] - _dot(inv[u], x[u]).astype(BF16) for u in idx]
    rhs = [jnp.concatenate([k[u] * (beta[u] * eg[u]).astype(BF16), v_ref[at[u]] * beta[u].astype(BF16)], axis=1)
           for u in idx]
    wu = [_dot(inv[u], rhs[u]).astype(BF16) for u in idx]
    aw = [_dot(attn_b[u], wu[u]) for u in idx]
    pn = [_dot_tn(k[u] * ed[u].astype(BF16), wu[u]) for u in idx]
    q_eff = [q_ref[at[u]].astype(F32) * eg[u] - aw[u][:, 0:HEAD_W] for u in idx]
    state = [state_ref[h] for h in heads]
    for ci in chunks:
        us = [ci * N_HEADS + h for h in heads]
        r = [_dot(jnp.concatenate([pn[u][:, 0:HEAD_W], q_eff[u]], axis=0).astype(BF16), state[h].astype(BF16))
             for h, u in zip(heads, us)]
        state = [state[h] * eg[u][c - 1:c, :] + pn[u][:, HEAD_W:2 * HEAD_W] - r[h][0:DK_B] for h, u in zip(heads, us)]
        for h, u in zip(heads, us):
            o = r[h][DK_B:DK_B + c] + aw[u][:, HEAD_W:2 * HEAD_W]
            z = z_ref[at[u]].astype(F32)
            o_ref[at[u]] = (_rmsnorm(o, on_ref[...]) * _silu(z)).astype(BF16)
    for h in heads:
        state_ref[h] = state[h]


def _gdn_constants():
    c = GDN_CHUNK
    r = np.arange(c)
    tril = (r[None, :] <= r[:, None]).astype(np.float32)
    tril3 = np.concatenate([tril, tril, tril], axis=1)
    levels = [np.eye(c, dtype=np.float32)]
    s = 1
    while s < c:
        same = (r[:, None] ^ r[None, :]) < 2 * s
        levels.append((same & ((r[:, None] & s) != 0) & ((r[None, :] & s) == 0)).astype(np.float32))
        s *= 2
    return jnp.asarray(tril3, BF16), jnp.asarray(np.stack(levels), BF16)


def _gdn(proj_b, small, alog_row, dtb_row, onorm):
    bsz, seq, _ = proj_b.shape
    c = GDN_CHUNK
    step_rows = GDN_STEP_CHUNKS * c
    assert c == LANE and seq % step_rows == 0
    tril3, levels = _gdn_constants()
    blk = lambda col: pl.BlockSpec((1, step_rows, D_MODEL), lambda b, s: (b, s, col))
    return pl.pallas_call(
        _gdn_kernel,
        grid=(bsz, seq // step_rows),
        in_specs=[
            blk(0), blk(1), blk(2), blk(3),
            pl.BlockSpec((1, step_rows, LANE), lambda b, s: (b, s, 0)),
            pl.BlockSpec((1, LANE), lambda b, s: (0, 0)),
            pl.BlockSpec((1, LANE), lambda b, s: (0, 0)),
            pl.BlockSpec((1, HEAD_W), lambda b, s: (0, 0)),
            _resident(tril3.shape), _resident(levels.shape),
        ],
        out_specs=pl.BlockSpec((1, step_rows, D_MODEL), lambda b, s: (b, s, 0)),
        out_shape=jax.ShapeDtypeStruct((bsz, seq, D_MODEL), BF16),
        scratch_shapes=[pltpu.VMEM((N_HEADS, DK_B, HEAD_W), F32)],
        compiler_params=_params(("parallel", "arbitrary")),
        name="gated_delta_rule",
    )(proj_b, proj_b, proj_b, proj_b, small, alog_row, dtb_row, onorm, tril3, levels)


def _merge_kernel(x_ref, oa_ref, ob_ref, ga_ref, gb_ref, wa_ref, wb_ref, wo_ref, o_ref):
    ya = _dot(oa_ref[...], wa_ref[...])
    yb = _dot(ob_ref[...], wb_ref[...])
    mixed = _sigmoid(ga_ref[...].astype(F32)) * ya + _sigmoid(gb_ref[...].astype(F32)) * yb
    o_ref[...] = x_ref[...] + _dot(mixed.astype(BF16), wo_ref[...])


def _merge(x2, oa, ob, proj_b, wa, wb, wo):
    rows = x2.shape[0]
    gate_block = 4
    row_spec = pl.BlockSpec((MERGE_ROWS, D_MODEL), lambda i: (i, 0))
    return pl.pallas_call(
        _merge_kernel,
        grid=(rows // MERGE_ROWS,),
        in_specs=[
            row_spec, row_spec, row_spec,
            pl.BlockSpec((MERGE_ROWS, D_MODEL), lambda i: (i, gate_block)),
            pl.BlockSpec((MERGE_ROWS, D_MODEL), lambda i: (i, gate_block + 1)),
            _resident((D_MODEL, D_MODEL)), _resident((D_MODEL, D_MODEL)), _resident((D_MODEL, D_MODEL)),
        ],
        out_specs=row_spec,
        out_shape=jax.ShapeDtypeStruct((rows, D_MODEL), F32),
        compiler_params=_params(("parallel",)),
        name="branch_merge",
    )(x2, oa, ob, proj_b, proj_b, wa, wb, wo)


def _ffn_ple_kernel(x_ref, halo_ref, p_ref, fg_ref, wup_ref, cw_ref, cb_ref, wdn_ref, pg_ref, wg_ref, wp_ref,
                    o_ref, act_ref, *, seq):
    rows = x_ref.shape[0]
    x = x_ref[...]
    fg = fg_ref[...]
    keep = jnp.where((pl.program_id(0) * rows) % seq == 0, 0.0, 1.0)
    h_ext = jnp.concatenate([_rmsnorm(halo_ref[...], fg) * keep, _rmsnorm(x, fg)], axis=0).astype(BF16)
    off = HALO - (CONV_FF - 1)

    def conv(u, lo):
        last = CONV_FF - 1
        y = cb_ref[:, lo:lo + FFN_COLS] + cw_ref[last:CONV_FF, lo:lo + FFN_COLS] * u[off + last:off + last + rows]
        for tap in range(last):
            y = y + cw_ref[tap:tap + 1, lo:lo + FFN_COLS] * u[off + tap:off + tap + rows]
        return y

    for c in range(D_FF // FFN_COLS):
        lo = c * FFN_COLS
        yg = conv(_dot(h_ext, wup_ref[:, lo:lo + FFN_COLS]), lo)
        yv = conv(_dot(h_ext, wup_ref[:, D_FF + lo:D_FF + lo + FFN_COLS]), D_FF + lo)
        act_ref[:, lo:lo + FFN_COLS] = (_silu(yg) * yv).astype(BF16)

    x1 = x + _dot(act_ref[...], wdn_ref[...])
    h2 = _rmsnorm(x1, pg_ref[...]).astype(BF16)
    gate = _sigmoid(_dot(h2, wg_ref[...]))
    o_ref[...] = x1 + gate * _dot(p_ref[...].astype(BF16), wp_ref[...])


def _ffn_ple(x2, p2, fg, wup, cw, cb, wdn, pg, wg, wp, seq):
    rows = x2.shape[0]
    assert seq % FFN_ROWS == 0 and D_FF % FFN_COLS == 0
    halo_blocks = FFN_ROWS // HALO
    return pl.pallas_call(
        functools.partial(_ffn_ple_kernel, seq=seq),
        grid=(rows // FFN_ROWS,),
        in_specs=[
            pl.BlockSpec((FFN_ROWS, D_MODEL), lambda i: (i, 0)),
            pl.BlockSpec((HALO, D_MODEL), lambda i: (jnp.maximum(i * halo_blocks - 1, 0), 0)),
            pl.BlockSpec((FFN_ROWS, D_PLE), lambda i: (i, 0)),
            _resident((1, D_MODEL)),
            _resident((D_MODEL, 2 * D_FF)),
            _resident((CONV_FF, 2 * D_FF)),
            _resident((1, 2 * D_FF)),
            _resident((D_FF, D_MODEL)),
            _resident((1, D_MODEL)),
            _resident((D_MODEL, D_MODEL)),
            _resident((D_PLE, D_MODEL)),
        ],
        out_specs=pl.BlockSpec((FFN_ROWS, D_MODEL), lambda i: (i, 0)),
        out_shape=jax.ShapeDtypeStruct((rows, D_MODEL), F32),
        scratch_shapes=[pltpu.VMEM((FFN_ROWS, D_FF), BF16)],
        compiler_params=_params(("arbitrary",)),
        name="conv_ffn_ple",
    )(x2, x2, p2, fg, wup, cw, cb, wdn, pg, wg, wp)


def kernel(x, p, w_in, attn_norm, qnorm_a, knorm_a, lam_a, subln_a, rel_bias, conv_b, a_log_b, dt_bias_b, onorm_b, w_br_a, w_br_b, w_out, ffn_norm, w_up, conv_ff, conv_ff_bias, w_down, ple_norm, w_ple_gate, w_ple_proj):
    bsz, seq, _ = x.shape
    depth = w_in.shape[0]
    rows = bsz * seq
    assert rows % PROJ_ROWS == 0 and seq % ATT_BLOCK == 0 and rows % MERGE_ROWS == 0

    n_a = 3 * D_MODEL
    n_b = 4 * D_MODEL
    w_in_b = w_in.astype(BF16)
    rel_bias_t = rel_bias.astype(F32).T * LOG2E
    bias_diag, bias_corner = _bias_tiles(rel_bias_t)
    n_maps = D_MODEL // HEAD_DIM_A

    x2 = x.reshape(rows, D_MODEL).astype(F32)
    for i in range(depth):
        lam_init = 0.8 - 0.6 * math.exp(-0.3 * i)
        w_a = w_in_b[i, :, 0:n_a]
        w_b = jnp.concatenate([w_in_b[i, :, n_a:n_a + n_b], w_in_b[i, :, n_a + n_b + 2 * N_HEADS:]], axis=1)
        w_small = jnp.pad(w_in_b[i, :, n_a + n_b:n_a + n_b + 2 * N_HEADS], ((0, 0), (0, LANE - 2 * N_HEADS)))
        gains = jnp.stack([
            jnp.tile(qnorm_a[i].astype(F32), n_maps) * (HEAD_DIM_A ** -0.5 * LOG2E),
            jnp.tile(knorm_a[i].astype(F32), n_maps),
            jnp.ones((D_MODEL,), F32),
        ]).reshape(3, 1, D_MODEL)
        g_attn = attn_norm[i].astype(F32).reshape(1, D_MODEL)

        qkv = _attn_proj(x2, g_attn, w_a, gains)
        proj_b, small = _gdn_proj(x2, g_attn, w_b, w_small, conv_b[i].astype(F32), seq)

        oa = _attention(qkv.reshape(bsz, seq, n_a), bias_diag, bias_corner, rel_bias_t, lam_a[i].astype(F32),
                        subln_a[i].astype(F32).reshape(1, HEAD_W), jnp.full((1,), lam_init, F32))
        head_lanes = (N_HEADS, LANE - 2 * N_HEADS)
        ob = _gdn(proj_b.reshape(bsz, seq, -1), small.reshape(bsz, seq, LANE),
                  jnp.pad(a_log_b[i].astype(F32), head_lanes).reshape(1, LANE),
                  jnp.pad(dt_bias_b[i].astype(F32), head_lanes).reshape(1, LANE),
                  onorm_b[i].astype(F32).reshape(1, HEAD_W))

        x2 = _merge(x2, oa.reshape(rows, D_MODEL), ob.reshape(rows, D_MODEL), proj_b,
                    w_br_a[i].astype(BF16), w_br_b[i].astype(BF16), w_out[i].astype(BF16))

        x2 = _ffn_ple(x2, p[i].reshape(rows, D_PLE), ffn_norm[i].astype(F32).reshape(1, D_MODEL),
                      w_up[i].astype(BF16), conv_ff[i].astype(F32), conv_ff_bias[i].astype(F32).reshape(1, 2 * D_FF),
                      w_down[i].astype(BF16), ple_norm[i].astype(F32).reshape(1, D_MODEL),
                      w_ple_gate[i].astype(BF16), w_ple_proj[i].astype(BF16), seq)
    return x2.reshape(bsz, seq, D_MODEL).astype(x.dtype)
```

```python
import functools
import math

import numpy as np
import jax
import jax.numpy as jnp
from jax import lax
from jax.experimental import pallas as pl
from jax.experimental.pallas import tpu as pltpu

F32 = jnp.float32
BF16 = jnp.bfloat16

D_MODEL = 1024
N_HEADS = 8
HEAD_W = 128
HEAD_DIM_A = 64
DK_B = 128
CONV_B = 4
CHUNK_B = 64
D_FF = 2816
CONV_FF = 3
D_PLE = 256
N_BUCKETS = 32
MAX_DISTANCE = 128
EPS = 1e-6
NEG_INF = -1e30

LANE = 128
SUBLANE = 8
VMEM_LIMIT_BYTES = 56 * 1024 * 1024

PROJ_ROWS = 1024
PROJ_COLS = 1024
PROJ_SUB = 256
GDN_PROJ_ROWS = 512
ATT_BLOCK = 512
ATT_HEADS = 8
ATT_ONES = 2 * SUBLANE
ATT_QSPLIT = 1
ATT_SKEW = 1
ATT_CORNER = 128
LOG2E = math.log2(math.e)
GDN_CHUNK = 128
GDN_STEP_CHUNKS = 4
MERGE_ROWS = 1024
FFN_ROWS = 512
FFN_COLS = 256
HALO = SUBLANE
PROJ_HALO = 2 * SUBLANE


def _bucket_thresholds():
    n = np.arange(0, 4 * MAX_DISTANCE)
    max_exact = N_BUCKETS // 2
    nf = np.maximum(n, 1).astype(np.float32)
    large = max_exact + (np.log(nf / np.float32(max_exact)) / np.float32(math.log(MAX_DISTANCE / max_exact))
                         * np.float32(N_BUCKETS - max_exact)).astype(np.int32)
    bucket = np.where(n < max_exact, n, np.minimum(large, N_BUCKETS - 1))
    return [int(np.argmax(bucket >= b)) for b in range(1, N_BUCKETS)]


BUCKET_LO = _bucket_thresholds()


def _params(semantics):
    return pltpu.CompilerParams(dimension_semantics=semantics, vmem_limit_bytes=VMEM_LIMIT_BYTES)


def _resident(shape):
    zeros = (0,) * len(shape)
    return pl.BlockSpec(shape, lambda *_: zeros, pipeline_mode=pl.Buffered(1))


def _rmsnorm(x, g):
    ms = jnp.mean(x * x, axis=-1, keepdims=True)
    return x * lax.rsqrt(ms + EPS) * g


def _sigmoid(x):
    return 1.0 / (1.0 + jnp.exp2(x * (-LOG2E)))


def _silu(x):
    return x * _sigmoid(x)


def _dot(a, b):
    return jnp.dot(a, b, preferred_element_type=F32)


def _dot_nt(a, b):
    return lax.dot_general(a, b, (((1,), (1,)), ((), ())), preferred_element_type=F32)


def _dot_tn(a, b):
    return lax.dot_general(a, b, (((0,), (0,)), ((), ())), preferred_element_type=F32)


def _attn_proj_kernel(x_ref, g_ref, w_ref, gain_ref, gsum_ref, o_ref, h_ref):
    j = pl.program_id(1)

    @pl.when(j == 0)
    def _():
        h_ref[...] = _rmsnorm(x_ref[...], g_ref[...]).astype(BF16)

    @pl.when(j < 2)
    def _():
        n_sub = PROJ_COLS // PROJ_SUB

        def project(sub):
            return _dot(h_ref[...], w_ref[:, sub * PROJ_SUB:(sub + 1) * PROJ_SUB])

        def normalise(sub, y):
            cols = slice(sub * PROJ_SUB, (sub + 1) * PROJ_SUB)
            ss = _dot((y * y).astype(BF16), gsum_ref[...])
            r = lax.rsqrt(ss * (1.0 / HEAD_DIM_A) + EPS)
            o_ref[:, cols] = (y * r * gain_ref[:, cols]).astype(BF16)

        y = project(0)
        for sub in range(n_sub):
            y_next = project(sub + 1) if sub + 1 < n_sub else None
            normalise(sub, y)
            y = y_next

    @pl.when(j == 2)
    def _():
        o_ref[...] = _dot(h_ref[...], w_ref[...]).astype(BF16)


def _attn_proj(x2, g, w, gains):
    rows = x2.shape[0]
    assert PROJ_SUB % HEAD_DIM_A == 0
    group = np.arange(PROJ_SUB) // HEAD_DIM_A
    gsum = (group[:, None] == group[None, :]).astype(np.float32)
    return pl.pallas_call(
        _attn_proj_kernel,
        grid=(rows // PROJ_ROWS, 3),
        in_specs=[
            pl.BlockSpec((PROJ_ROWS, D_MODEL), lambda i, j: (i, 0)),
            pl.BlockSpec((1, D_MODEL), lambda i, j: (0, 0)),
            pl.BlockSpec((D_MODEL, PROJ_COLS), lambda i, j: (0, j)),
            pl.BlockSpec((None, 1, PROJ_COLS), lambda i, j: (j, 0, 0)),
            pl.BlockSpec((PROJ_SUB, PROJ_SUB), lambda i, j: (0, 0)),
        ],
        out_specs=pl.BlockSpec((PROJ_ROWS, PROJ_COLS), lambda i, j: (i, j)),
        out_shape=jax.ShapeDtypeStruct((rows, 3 * D_MODEL), BF16),
        scratch_shapes=[pltpu.VMEM((PROJ_ROWS, D_MODEL), BF16)],
        compiler_params=_params(("parallel", "arbitrary")),
        name="attn_proj",
    )(x2, g, w, gains, jnp.asarray(gsum, BF16))


def _gdn_proj_kernel(x_ref, halo_ref, g_ref, w_ref, wsm_ref, cw_ref, o_ref, osm_ref, h_ref, u_ref, *, seq):
    rows = x_ref.shape[0]
    g = g_ref[...]
    keep = jnp.where((pl.program_id(0) * rows) % seq == 0, 0.0, 1.0)
    h_ref[0:PROJ_HALO, :] = (_rmsnorm(halo_ref[...], g) * keep).astype(BF16)
    h = _rmsnorm(x_ref[...], g).astype(BF16)
    h_ref[PROJ_HALO:, :] = h
    osm_ref[...] = _dot(h, wsm_ref[...])

    n_conv_cols = cw_ref.shape[1]
    n_conv = n_conv_cols // PROJ_SUB
    n_plain = (w_ref.shape[1] - n_conv_cols) // PROJ_SUB
    off = PROJ_HALO - (CONV_B - 1)

    def project_conv(sub):
        u_ref[sub % 2] = _dot(h_ref[...], w_ref[:, sub * PROJ_SUB:(sub + 1) * PROJ_SUB])

    def project_plain(sub):
        cols = slice(n_conv_cols + sub * PROJ_SUB, n_conv_cols + (sub + 1) * PROJ_SUB)
        o_ref[:, cols] = _dot(h_ref[PROJ_HALO:, :], w_ref[:, cols]).astype(BF16)

    def epilogue(sub):
        kind = sub * PROJ_SUB // D_MODEL
        u = u_ref.at[sub % 2]
        for hh in range(PROJ_SUB // HEAD_W):
            lo = hh * HEAD_W
            col = sub * PROJ_SUB + lo
            last = CONV_B - 1
            y = cw_ref[last:CONV_B, col:col + HEAD_W] * u[off + last:off + last + rows, lo:lo + HEAD_W]
            for tap in range(last):
                y = y + cw_ref[tap:tap + 1, col:col + HEAD_W] * u[off + tap:off + tap + rows, lo:lo + HEAD_W]
            y = _silu(y)
            if kind < 2:
                y = y * (lax.rsqrt(jnp.sum(y * y, axis=-1, keepdims=True) + EPS) * (DK_B ** -0.5 if kind == 0 else 1.0))
            o_ref[:, col:col + HEAD_W] = y.astype(BF16)

    project_conv(0)
    for sub in range(max(n_conv, n_plain)):
        if sub + 1 < n_conv:
            project_conv(sub + 1)
        if sub < n_plain:
            project_plain(sub)
        if sub < n_conv:
            epilogue(sub)


def _gdn_proj(x2, g, w, wsm, conv_w, seq):
    rows = x2.shape[0]
    n_cols = w.shape[1]
    assert seq % GDN_PROJ_ROWS == 0 and GDN_PROJ_ROWS % PROJ_HALO == 0
    assert conv_w.shape[1] % PROJ_SUB == 0 and (n_cols - conv_w.shape[1]) % PROJ_SUB == 0 and D_MODEL % PROJ_SUB == 0
    halo_blocks = GDN_PROJ_ROWS // PROJ_HALO
    return pl.pallas_call(
        functools.partial(_gdn_proj_kernel, seq=seq),
        grid=(rows // GDN_PROJ_ROWS,),
        in_specs=[
            pl.BlockSpec((GDN_PROJ_ROWS, D_MODEL), lambda i: (i, 0)),
            pl.BlockSpec((PROJ_HALO, D_MODEL), lambda i: (jnp.maximum(i * halo_blocks - 1, 0), 0)),
            _resident((1, D_MODEL)),
            _resident((D_MODEL, n_cols)),
            _resident((D_MODEL, LANE)),
            _resident(conv_w.shape),
        ],
        out_specs=[
            pl.BlockSpec((GDN_PROJ_ROWS, n_cols), lambda i: (i, 0)),
            pl.BlockSpec((GDN_PROJ_ROWS, LANE), lambda i: (i, 0)),
        ],
        out_shape=[
            jax.ShapeDtypeStruct((rows, n_cols), BF16),
            jax.ShapeDtypeStruct((rows, LANE), F32),
        ],
        scratch_shapes=[pltpu.VMEM((PROJ_HALO + GDN_PROJ_ROWS, D_MODEL), BF16),
                        pltpu.VMEM((2, PROJ_HALO + GDN_PROJ_ROWS, PROJ_SUB), F32)],
        compiler_params=_params(("parallel",)),
        name="gdn_proj",
    )(x2, x2, g, w, wsm, conv_w)


def _bias_tiles_kernel(tab_ref, diag_ref, corner_ref):
    h = pl.program_id(0)
    t = diag_ref.shape[1]
    cn = corner_ref.shape[1]

    def bias_of(rel):
        b = jnp.full(rel.shape, tab_ref[h, 0], F32)
        for bk in range(1, N_BUCKETS):
            b = jnp.where(rel >= BUCKET_LO[bk - 1], tab_ref[h, bk], b)
        return b

    rel = lax.broadcasted_iota(jnp.int32, (t, t), 1) - lax.broadcasted_iota(jnp.int32, (t, t), 0)
    diag_ref[0] = jnp.where(rel >= 0, bias_of(rel), NEG_INF)
    rel_c = (lax.broadcasted_iota(jnp.int32, (cn, cn), 1) - lax.broadcasted_iota(jnp.int32, (cn, cn), 0)) + cn
    corner_ref[0] = bias_of(rel_c) - tab_ref[h, N_BUCKETS - 1]


def _bias_tiles(rel_bias_t):
    t = ATT_BLOCK
    assert ATT_CORNER + 1 >= BUCKET_LO[-1], "entries outside the corner must sit in the last bucket"
    return pl.pallas_call(
        _bias_tiles_kernel,
        grid=(N_HEADS,),
        in_specs=[pl.BlockSpec(memory_space=pltpu.SMEM)],
        out_specs=[
            pl.BlockSpec((1, t, t), lambda h: (h, 0, 0)),
            pl.BlockSpec((1, ATT_CORNER, ATT_CORNER), lambda h: (h, 0, 0)),
        ],
        out_shape=[
            jax.ShapeDtypeStruct((N_HEADS, t, t), F32),
            jax.ShapeDtypeStruct((N_HEADS, ATT_CORNER, ATT_CORNER), F32),
        ],
        compiler_params=_params(("parallel",)),
        name="bias_tiles",
    )(rel_bias_t)


def _attn_kernel(qi_tab, kj_tab, q_ref, k_ref, v_ref, diag_ref, corner_ref, tab_ref, lam_ref, sg_ref, li_ref,
                 o_ref, qs_ref, vt_ref, m_ref, acc_ref):
    hg = pl.program_id(0)
    step = pl.program_id(2)
    qi = qi_tab[step]
    kj = kj_tab[step]
    t = q_ref.shape[1]
    cn = corner_ref.shape[1]
    heads = range(ATT_HEADS)
    cols = [slice(g * HEAD_W, (g + 1) * HEAD_W) for g in heads]

    @pl.when(kj == 0)
    def _():
        lane = lax.broadcasted_iota(jnp.int32, (t, HEAD_W), 1)
        for g in heads:
            q = q_ref[0, :, cols[g]]
            zero = jnp.zeros_like(q)
            qs_ref[g, 0] = jnp.where(lane < HEAD_DIM_A, q, zero)
            qs_ref[g, 1] = jnp.where(lane >= HEAD_DIM_A, q, zero)
            vt_ref[g, HEAD_W:HEAD_W + ATT_ONES, :] = jnp.ones((ATT_ONES, t), BF16)
        m_ref[...] = jnp.full(m_ref.shape, NEG_INF, F32)
        acc_ref[...] = jnp.zeros(acc_ref.shape, F32)

    for g in heads:
        vt_ref[g, 0:HEAD_W, :] = v_ref[0, :, cols[g]].astype(F32).T.astype(BF16)
    delta = qi - kj

    def sweep(bias_fn, c):
        tq = t // ATT_QSPLIT
        units = [(g, j, slice(qp * tq, (qp + 1) * tq)) for g in heads for j in range(2) for qp in range(ATT_QSPLIT)]
        n = len(units)
        s, m_prev, m_new, p = {}, {}, {}, {}

        def logits(i):
            g, j, qcols = units[i]
            s[i] = bias_fn(_dot_nt(k_ref[0, :, cols[g]], qs_ref[g, j, qcols, :]), g, qcols)

        def softmax(i):
            g, j, qcols = units[i]
            m_prev[i] = m_ref[g, j, :, qcols]
            col_max = jnp.broadcast_to(jnp.max(s[i], axis=0, keepdims=True), (SUBLANE, tq))
            m_new[i] = jnp.maximum(m_prev[i], col_max + c[g])
            p[i] = jnp.exp2(s.pop(i) - (m_new[i][0:1, :] - c[g])).astype(BF16)

        def values(i):
            g, j, qcols = units[i]
            pv = _dot(vt_ref[g], p.pop(i))
            alpha = jnp.exp2(m_prev.pop(i) - m_new[i])
            acc_ref[g, j, :, qcols] = alpha[0:1, :] * acc_ref[g, j, :, qcols] + pv
            m_ref[g, j, :, qcols] = m_new.pop(i)

        for i in range(n + 2 * ATT_SKEW):
            if i < n:
                logits(i)
            if 0 <= i - ATT_SKEW < n:
                softmax(i - ATT_SKEW)
            if 0 <= i - 2 * ATT_SKEW < n:
                values(i - 2 * ATT_SKEW)

    @pl.when(delta >= 1)
    def _():
        near = jnp.where(delta == 1, 1.0, 0.0)

        def bias_fn(s, g, qcols):
            if qcols.start != 0:
                return s
            bottom = jnp.concatenate([s[t - cn:t, 0:cn] + near * corner_ref[g], s[t - cn:t, cn:]], axis=1)
            return jnp.concatenate([s[0:t - cn, :], bottom], axis=0)

        sweep(bias_fn, [tab_ref[hg * ATT_HEADS + g, N_BUCKETS - 1] for g in heads])

    @pl.when(delta == 0)
    def _():
        sweep(lambda s, g, qcols: s + diag_ref[g, :, qcols], [0.0] * ATT_HEADS)
        lq = lam_ref[...]
        lam_init = li_ref[0]
        lam = (jnp.exp(jnp.sum(lq[0:1] * lq[1:2], axis=-1, keepdims=True))
               - jnp.exp(jnp.sum(lq[2:3] * lq[3:4], axis=-1, keepdims=True)) + lam_init)
        for g in heads:
            a0 = acc_ref[g, 0]
            a1 = acc_ref[g, 1]
            o_t = (a0[0:HEAD_W] / a0[HEAD_W:HEAD_W + 1] - lam * (a1[0:HEAD_W] / a1[HEAD_W:HEAD_W + 1]))
            o_ref[0, :, cols[g]] = (_rmsnorm(o_t.T, sg_ref[...]) * (1.0 - lam_init)).astype(BF16)


def _attention(qkv, bias_diag, bias_corner, rel_bias_t, lam_a, subln, lam_init):
    bsz, seq, _ = qkv.shape
    t = ATT_BLOCK
    nq = seq // t
    assert ATT_CORNER <= t and N_HEADS % ATT_HEADS == 0
    n_groups = N_HEADS // ATT_HEADS
    gw = ATT_HEADS * HEAD_W
    pairs =[(qi, kj) for qi in range(nq) for kj in range(qi + 1)]
    qi_tab = jnp.asarray([p[0] for p in pairs], jnp.int32)
    kj_tab = jnp.asarray([p[1] for p in pairs], jnp.int32)
    grid_spec = pltpu.PrefetchScalarGridSpec(
        num_scalar_prefetch=2,
        grid=(n_groups, bsz, len(pairs)),
        in_specs=[
            pl.BlockSpec((1, t, gw), lambda h, b, s, qt, kt: (b, qt[s], h)),
            pl.BlockSpec((1, t, gw), lambda h, b, s, qt, kt: (b, kt[s], n_groups + h)),
            pl.BlockSpec((1, t, gw), lambda h, b, s, qt, kt: (b, kt[s], 2 * n_groups + h)),
            pl.BlockSpec((ATT_HEADS, t, t), lambda h, b, s, qt, kt: (h, 0, 0)),
            pl.BlockSpec((ATT_HEADS, ATT_CORNER, ATT_CORNER), lambda h, b, s, qt, kt: (h, 0, 0)),
            pl.BlockSpec(memory_space=pltpu.SMEM),
            pl.BlockSpec((4, HEAD_DIM_A), lambda h, b, s, qt, kt: (0, 0)),
            pl.BlockSpec((1, HEAD_W), lambda h, b, s, qt, kt: (0, 0)),
            pl.BlockSpec(memory_space=pltpu.SMEM),
        ],
        out_specs=pl.BlockSpec((1, t, gw), lambda h, b, s, qt, kt: (b, qt[s], h)),
        scratch_shapes=[
            pltpu.VMEM((ATT_HEADS, 2, t, HEAD_W), BF16),
            pltpu.VMEM((ATT_HEADS, HEAD_W + ATT_ONES, t), BF16),
            pltpu.VMEM((ATT_HEADS, 2, SUBLANE, t), F32),
            pltpu.VMEM((ATT_HEADS, 2, HEAD_W + ATT_ONES, t), F32),
        ],
    )
    return pl.pallas_call(
        _attn_kernel,
        grid_spec=grid_spec,
        out_shape=jax.ShapeDtypeStruct((bsz, seq, D_MODEL), BF16),
        compiler_params=_params(("parallel", "parallel", "arbitrary")),
        name="diff_attention",
    )(qi_tab, kj_tab, qkv, qkv, qkv, bias_diag, bias_corner, rel_bias_t, lam_a, subln, lam_init)


def _split(x, n_pieces):
    pieces = []
    for _ in range(n_pieces):
        piece = x.astype(BF16)
        pieces.append(piece)
        x = x - piece.astype(F32)
    return pieces


def _gdn_kernel(q_ref, k_ref, v_ref, z_ref, sm_ref, alog_ref, dtb_ref, on_ref, tril_ref, lvl_ref,
                o_ref, state_ref):
    c = GDN_CHUNK

    @pl.when(pl.program_id(1) == 0)
    def _():
        state_ref[...] = jnp.zeros(state_ref.shape, F32)

    chunks = range(q_ref.shape[1] // c)
    rows = [slice(ci * c, (ci + 1) * c) for ci in chunks]

    beta_all, gc_all, eg_all, ed_all = [], [], [], []
    for ci in chunks:
        sm = sm_ref[0, rows[ci], :]
        xa = sm + dtb_ref[...]
        softplus = jnp.maximum(xa, 0.0) + jnp.log1p(jnp.exp(-jnp.abs(xa)))
        g_all = -jnp.exp(alog_ref[...]) * softplus
        gc = _dot(tril_ref[...], jnp.concatenate(_split(g_all, 3), axis=0))
        beta_all.append(_sigmoid(sm))
        gc_all.append(gc)
        eg_all.append(jnp.exp(gc))
        ed_all.append(jnp.exp(gc[c - 1:c, :] - gc))

    ii = lax.broadcasted_iota(jnp.int32, (c, c), 0)
    jj = lax.broadcasted_iota(jnp.int32, (c, c), 1)
    tril = ii >= jj
    strict = ii > jj
    n_levels = lvl_ref.shape[0] - 1

    heads = range(N_HEADS)
    cols = [slice(h * HEAD_W, (h + 1) * HEAD_W) for h in heads]
    units = [(ci, h) for ci in chunks for h in heads]
    idx = range(len(units))
    lane_b = [slice(h, h + 1) for _, h in units]
    lane_g = [slice(N_HEADS + h, N_HEADS + h + 1) for _, h in units]
    at = [(0, rows[ci], cols[h]) for ci, h in units]
    beta = [beta_all[units[u][0]][:, lane_b[u]] for u in idx]
    gcr = [jnp.broadcast_to(gc_all[units[u][0]][:, lane_g[u]], (c, c)) for u in idx]
    eg = [eg_all[units[u][0]][:, lane_g[u]] for u in idx]
    ed = [ed_all[units[u][0]][:, lane_g[u]] for u in idx]
    k = [k_ref[at[u]] for u in idx]
    kq = [_dot_nt(jnp.concatenate([k[u], q_ref[at[u]]], axis=0), k[u]) for u in idx]
    decay = [jnp.where(tril, jnp.exp(gcr[u] - gcr[u].T), 0.0) for u in idx]
    a_b = [jnp.where(strict, kq[u][0:c] * (decay[u] * beta[u]), 0.0).astype(BF16) for u in idx]
    attn_b = [(kq[u][c:2 * c] * decay[u]).astype(BF16) for u in idx]
    inv = [lvl_ref[0] - a_b[u] * lvl_ref[1] for u in idx]
    for lv in range(2, n_levels + 1):
        x = [_dot(a_b[u] * lvl_ref[lv], inv[u]).astype(BF16) for u in idx]
        inv = [inv[u] - _dot(inv[u], x[u]).astype(BF16) for u in idx]
    rhs = [jnp.concatenate([k[u] * (beta[u] * eg[u]).astype(BF16), v_ref[at[u]] * beta[u].astype(BF16)], axis=1)
           for u in idx]
    wu = [_dot(inv[u], rhs[u]).astype(BF16) for u in idx]
    aw = [_dot(attn_b[u], wu[u]) for u in idx]
    pn = [_dot_tn(k[u] * ed[u].astype(BF16), wu[u]) for u in idx]
    q_eff = [q_ref[at[u]].astype(F32) * eg[u] - aw[u][:, 0:HEAD_W] for u in idx]
    state = [state_ref[h] for h in heads]
    for ci in chunks:
        us = [ci * N_HEADS + h for h in heads]
        r = [_dot(jnp.concatenate([pn[u][:, 0:HEAD_W], q_eff[u]], axis=0).astype(BF16), state[h].astype(BF16))
             for h, u in zip(heads, us)]
        state = [state[h] * eg[u][c - 1:c, :] + pn[u][:, HEAD_W:2 * HEAD_W] - r[h][0:DK_B] for h, u in zip(heads, us)]
        for h, u in zip(heads, us):
            o = r[h][DK_B:DK_B + c] + aw[u][:, HEAD_W:2 * HEAD_W]
            z = z_ref[at[u]].astype(F32)
            o_ref[at[u]] = (_rmsnorm(o, on_ref[...]) * _silu(z)).astype(BF16)
    for h in heads:
        state_ref[h] = state[h]


def _gdn_constants():
    c = GDN_CHUNK
    r = np.arange(c)
    tril = (r[None, :] <= r[:, None]).astype(np.float32)
    tril3 = np.concatenate([tril, tril, tril], axis=1)
    levels = [np.eye(c, dtype=np.float32)]
    s = 1
    while s < c:
        same = (r[:, None] ^ r[None, :]) < 2 * s
        levels.append((same & ((r[:, None] & s) != 0) & ((r[None, :] & s) == 0)).astype(np.float32))
        s *= 2
    return jnp.asarray(tril3, BF16), jnp.asarray(np.stack(levels), BF16)


def _gdn(proj_b, small, alog_row, dtb_row, onorm):
    bsz, seq, _ = proj_b.shape
    c = GDN_CHUNK
    step_rows = GDN_STEP_CHUNKS * c
    assert c == LANE and seq % step_rows == 0
    tril3, levels = _gdn_constants()
    blk = lambda col: pl.BlockSpec((1, step_rows, D_MODEL), lambda b, s: (b, s, col))
    return pl.pallas_call(
        _gdn_kernel,
        grid=(bsz, seq // step_rows),
        in_specs=[
            blk(0), blk(1), blk(2), blk(3),
            pl.BlockSpec((1, step_rows, LANE), lambda b, s: (b, s, 0)),
            pl.BlockSpec((1, LANE), lambda b, s: (0, 0)),
            pl.BlockSpec((1, LANE), lambda b, s: (0, 0)),
            pl.BlockSpec((1, HEAD_W), lambda b, s: (0, 0)),
            _resident(tril3.shape), _resident(levels.shape),
        ],
        out_specs=pl.BlockSpec((1, step_rows, D_MODEL), lambda b, s: (b, s, 0)),
        out_shape=jax.ShapeDtypeStruct((bsz, seq, D_MODEL), BF16),
        scratch_shapes=[pltpu.VMEM((N_HEADS, DK_B, HEAD_W), F32)],
        compiler_params=_params(("parallel", "arbitrary")),
        name="gated_delta_rule",
    )(proj_b, proj_b, proj_b, proj_b, small, alog_row, dtb_row, onorm, tril3, levels)


def _merge_kernel(x_ref, oa_ref, ob_ref, ga_ref, gb_ref, wa_ref, wb_ref, wo_ref, o_ref):
    ya = _dot(oa_ref[...], wa_ref[...])
    yb = _dot(ob_ref[...], wb_ref[...])
    mixed = _sigmoid(ga_ref[...].astype(F32)) * ya + _sigmoid(gb_ref[...].astype(F32)) * yb
    o_ref[...] = x_ref[...] + _dot(mixed.astype(BF16), wo_ref[...])


def _merge(x2, oa, ob, proj_b, wa, wb, wo):
    rows = x2.shape[0]
    gate_block = 4
    row_spec = pl.BlockSpec((MERGE_ROWS, D_MODEL), lambda i: (i, 0))
    return pl.pallas_call(
        _merge_kernel,
        grid=(rows // MERGE_ROWS,),
        in_specs=[
            row_spec, row_spec, row_spec,
            pl.BlockSpec((MERGE_ROWS, D_MODEL), lambda i: (i, gate_block)),
            pl.BlockSpec((MERGE_ROWS, D_MODEL), lambda i: (i, gate_block + 1)),
            _resident((D_MODEL, D_MODEL)), _resident((D_MODEL, D_MODEL)), _resident((D_MODEL, D_MODEL)),
        ],
        out_specs=row_spec,
        out_shape=jax.ShapeDtypeStruct((rows, D_MODEL), F32),
        compiler_params=_params(("parallel",)),
        name="branch_merge",
    )(x2, oa, ob, proj_b, proj_b, wa, wb, wo)


def _ffn_ple_kernel(x_ref, halo_ref, p_ref, fg_ref, wup_ref, cw_ref, cb_ref, wdn_ref, pg_ref, wg_ref, wp_ref,
                    o_ref, act_ref, *, seq):
    rows = x_ref.shape[0]
    x = x_ref[...]
    fg = fg_ref[...]
    keep = jnp.where((pl.program_id(0) * rows) % seq == 0, 0.0, 1.0)
    h_ext = jnp.concatenate([_rmsnorm(halo_ref[...], fg) * keep, _rmsnorm(x, fg)], axis=0).astype(BF16)
    off = HALO - (CONV_FF - 1)

    def conv(u, lo):
        last = CONV_FF - 1
        y = cb_ref[:, lo:lo + FFN_COLS] + cw_ref[last:CONV_FF, lo:lo + FFN_COLS] * u[off + last:off + last + rows]
        for tap in range(last):
            y = y + cw_ref[tap:tap + 1, lo:lo + FFN_COLS] * u[off + tap:off + tap + rows]
        return y

    for c in range(D_FF // FFN_COLS):
        lo = c * FFN_COLS
        yg = conv(_dot(h_ext, wup_ref[:, lo:lo + FFN_COLS]), lo)
        yv = conv(_dot(h_ext, wup_ref[:, D_FF + lo:D_FF + lo + FFN_COLS]), D_FF + lo)
        act_ref[:, lo:lo + FFN_COLS] = (_silu(yg) * yv).astype(BF16)

    x1 = x + _dot(act_ref[...], wdn_ref[...])
    h2 = _rmsnorm(x1, pg_ref[...]).astype(BF16)
    gate = _sigmoid(_dot(h2, wg_ref[...]))
    o_ref[...] = x1 + gate * _dot(p_ref[...].astype(BF16), wp_ref[...])


def _ffn_ple(x2, p2, fg, wup, cw, cb, wdn, pg, wg, wp, seq):
    rows = x2.shape[0]
    assert seq % FFN_ROWS == 0 and D_FF % FFN_COLS == 0
    halo_blocks = FFN_ROWS // HALO
    return pl.pallas_call(
        functools.partial(_ffn_ple_kernel, seq=seq),
        grid=(rows // FFN_ROWS,),
        in_specs=[
            pl.BlockSpec((FFN_ROWS, D_MODEL), lambda i: (i, 0)),
            pl.BlockSpec((HALO, D_MODEL), lambda i: (jnp.maximum(i * halo_blocks - 1, 0), 0)),
            pl.BlockSpec((FFN_ROWS, D_PLE), lambda i: (i, 0)),
            _resident((1, D_MODEL)),
            _resident((D_MODEL, 2 * D_FF)),
            _resident((CONV_FF, 2 * D_FF)),
            _resident((1, 2 * D_FF)),
            _resident((D_FF, D_MODEL)),
            _resident((1, D_MODEL)),
            _resident((D_MODEL, D_MODEL)),
            _resident((D_PLE, D_MODEL)),
        ],
        out_specs=pl.BlockSpec((FFN_ROWS, D_MODEL), lambda i: (i, 0)),
        out_shape=jax.ShapeDtypeStruct((rows, D_MODEL), F32),
        scratch_shapes=[pltpu.VMEM((FFN_ROWS, D_FF), BF16)],
        compiler_params=_params(("arbitrary",)),
        name="conv_ffn_ple",
    )(x2, x2, p2, fg, wup, cw, cb, wdn, pg, wg, wp)


def kernel(x, p, w_in, attn_norm, qnorm_a, knorm_a, lam_a, subln_a, rel_bias, conv_b, a_log_b, dt_bias_b, onorm_b, w_br_a, w_br_b, w_out, ffn_norm, w_up, conv_ff, conv_ff_bias, w_down, ple_norm, w_ple_gate, w_ple_proj):
    bsz, seq, _ = x.shape
    depth = w_in.shape[0]
    rows = bsz * seq
    assert rows % PROJ_ROWS == 0 and seq % ATT_BLOCK == 0 and rows % MERGE_ROWS == 0

    n_a = 3 * D_MODEL
    n_b = 4 * D_MODEL
    w_in_b = w_in.astype(BF16)
    rel_bias_t = rel_bias.astype(F32).T * LOG2E
    bias_diag, bias_corner = _bias_tiles(rel_bias_t)
    n_maps = D_MODEL // HEAD_DIM_A

    x2 = x.reshape(rows, D_MODEL).astype(F32)
    for i in range(depth):
        lam_init = 0.8 - 0.6 * math.exp(-0.3 * i)
        w_a = w_in_b[i, :, 0:n_a]
        w_b = jnp.concatenate([w_in_b[i, :, n_a:n_a + n_b], w_in_b[i, :, n_a + n_b + 2 * N_HEADS:]], axis=1)
        w_small = jnp.pad(w_in_b[i, :, n_a + n_b:n_a + n_b + 2 * N_HEADS], ((0, 0), (0, LANE - 2 * N_HEADS)))
        gains = jnp.stack([
            jnp.tile(qnorm_a[i].astype(F32), n_maps) * (HEAD_DIM_A ** -0.5 * LOG2E),
            jnp.tile(knorm_a[i].astype(F32), n_maps),
            jnp.ones((D_MODEL,), F32),
        ]).reshape(3, 1, D_MODEL)
        g_attn = attn_norm[i].astype(F32).reshape(1, D_MODEL)

        qkv = _attn_proj(x2, g_attn, w_a, gains)
        proj_b, small = _gdn_proj(x2, g_attn, w_b, w_small, conv_b[i].astype(F32), seq)

        oa = _attention(qkv.reshape(bsz, seq, n_a), bias_diag, bias_corner, rel_bias_t, lam_a[i].astype(F32),
                        subln_a[i].astype(F32).reshape(1, HEAD_W), jnp.full((1,), lam_init, F32))
        head_lanes = (N_HEADS, LANE - 2 * N_HEADS)
        ob = _gdn(proj_b.reshape(bsz, seq, -1), small.reshape(bsz, seq, LANE),
                  jnp.pad(a_log_b[i].astype(F32), head_lanes).reshape(1, LANE),
                  jnp.pad(dt_bias_b[i].astype(F32), head_lanes).reshape(1, LANE),
                  onorm_b[i].astype(F32).reshape(1, HEAD_W))

        x2 = _merge(x2, oa.reshape(rows, D_MODEL), ob.reshape(rows, D_MODEL), proj_b,
                    w_br_a[i].astype(BF16), w_br_b[i].astype(BF16), w_out[i].astype(BF16))

        x2 = _ffn_ple(x2, p[i].reshape(rows, D_PLE), ffn_norm[i].astype(F32).reshape(1, D_MODEL),
                      w_up[i].astype(BF16), conv_ff[i].astype(F32), conv_ff_bias[i].astype(F32).reshape(1, 2 * D_FF),
                      w_down[i].astype(BF16), ple_norm[i].astype(F32).reshape(1, D_MODEL),
                      w_ple_gate[i].astype(BF16), w_ple_proj[i].astype(BF16), seq)
    return x2.reshape(bsz, seq, D_MODEL).astype(x.dtype)
```

```python
import functools
import math

import numpy as np
import jax
import jax.numpy as jnp
from jax import lax
from jax.experimental import pallas as pl
from jax.experimental.pallas import tpu as pltpu

F32 = jnp.float32
BF16 = jnp.bfloat16

D_MODEL = 1024
N_HEADS = 8
HEAD_W = 128
HEAD_DIM_A = 64
DK_B = 128
CONV_B = 4
CHUNK_B = 64
D_FF = 2816
CONV_FF = 3
D_PLE = 256
N_BUCKETS = 32
MAX_DISTANCE = 128
EPS = 1e-6
NEG_INF = -1e30

LANE = 128
SUBLANE = 8
VMEM_LIMIT_BYTES = 56 * 1024 * 1024

PROJ_ROWS = 1024
PROJ_COLS = 1024
PROJ_SUB = 256
GDN_PROJ_ROWS = 512
ATT_BLOCK = 512
ATT_HEADS = 8
ATT_ONES = 2 * SUBLANE
ATT_QSPLIT = 2
ATT_SKEW = 1
ATT_CORNER = 128
LOG2E = math.log2(math.e)
GDN_CHUNK = 128
GDN_STEP_CHUNKS = 4
MERGE_ROWS = 1024
FFN_ROWS = 512
FFN_COLS = 256
HALO = SUBLANE
PROJ_HALO = 2 * SUBLANE


def _bucket_thresholds():
    n = np.arange(0, 4 * MAX_DISTANCE)
    max_exact = N_BUCKETS // 2
    nf = np.maximum(n, 1).astype(np.float32)
    large = max_exact + (np.log(nf / np.float32(max_exact)) / np.float32(math.log(MAX_DISTANCE / max_exact))
                         * np.float32(N_BUCKETS - max_exact)).astype(np.int32)
    bucket = np.where(n < max_exact, n, np.minimum(large, N_BUCKETS - 1))
    return [int(np.argmax(bucket >= b)) for b in range(1, N_BUCKETS)]


BUCKET_LO = _bucket_thresholds()


def _params(semantics):
    return pltpu.CompilerParams(dimension_semantics=semantics, vmem_limit_bytes=VMEM_LIMIT_BYTES)


def _resident(shape):
    zeros = (0,) * len(shape)
    return pl.BlockSpec(shape, lambda *_: zeros, pipeline_mode=pl.Buffered(1))


def _rmsnorm(x, g):
    ms = jnp.mean(x * x, axis=-1, keepdims=True)
    return x * lax.rsqrt(ms + EPS) * g


def _sigmoid(x):
    return 1.0 / (1.0 + jnp.exp2(x * (-LOG2E)))


def _silu(x):
    return x * _sigmoid(x)


def _dot(a, b):
    return jnp.dot(a, b, preferred_element_type=F32)


def _dot_nt(a, b):
    return lax.dot_general(a, b, (((1,), (1,)), ((), ())), preferred_element_type=F32)


def _dot_tn(a, b):
    return lax.dot_general(a, b, (((0,), (0,)), ((), ())), preferred_element_type=F32)


def _attn_proj_kernel(x_ref, g_ref, w_ref, gain_ref, gsum_ref, o_ref, h_ref):
    j = pl.program_id(1)

    @pl.when(j == 0)
    def _():
        h_ref[...] = _rmsnorm(x_ref[...], g_ref[...]).astype(BF16)

    @pl.when(j < 2)
    def _():
        n_sub = PROJ_COLS // PROJ_SUB

        def project(sub):
            return _dot(h_ref[...], w_ref[:, sub * PROJ_SUB:(sub + 1) * PROJ_SUB])

        def normalise(sub, y):
            cols = slice(sub * PROJ_SUB, (sub + 1) * PROJ_SUB)
            ss = _dot((y * y).astype(BF16), gsum_ref[...])
            r = lax.rsqrt(ss * (1.0 / HEAD_DIM_A) + EPS)
            o_ref[:, cols] = (y * r * gain_ref[:, cols]).astype(BF16)

        y = project(0)
        for sub in range(n_sub):
            y_next = project(sub + 1) if sub + 1 < n_sub else None
            normalise(sub, y)
            y = y_next

    @pl.when(j == 2)
    def _():
        o_ref[...] = _dot(h_ref[...], w_ref[...]).astype(BF16)


def _attn_proj(x2, g, w, gains):
    rows = x2.shape[0]
    assert PROJ_SUB % HEAD_DIM_A == 0
    group = np.arange(PROJ_SUB) // HEAD_DIM_A
    gsum = (group[:, None] == group[None, :]).astype(np.float32)
    return pl.pallas_call(
        _attn_proj_kernel,
        grid=(rows // PROJ_ROWS, 3),
        in_specs=[
            pl.BlockSpec((PROJ_ROWS, D_MODEL), lambda i, j: (i, 0)),
            pl.BlockSpec((1, D_MODEL), lambda i, j: (0, 0)),
            pl.BlockSpec((D_MODEL, PROJ_COLS), lambda i, j: (0, j)),
            pl.BlockSpec((None, 1, PROJ_COLS), lambda i, j: (j, 0, 0)),
            pl.BlockSpec((PROJ_SUB, PROJ_SUB), lambda i, j: (0, 0)),
        ],
        out_specs=pl.BlockSpec((PROJ_ROWS, PROJ_COLS), lambda i, j: (i, j)),
        out_shape=jax.ShapeDtypeStruct((rows, 3 * D_MODEL), BF16),
        scratch_shapes=[pltpu.VMEM((PROJ_ROWS, D_MODEL), BF16)],
        compiler_params=_params(("parallel", "arbitrary")),
        name="attn_proj",
    )(x2, g, w, gains, jnp.asarray(gsum, BF16))


def _gdn_proj_kernel(x_ref, halo_ref, g_ref, w_ref, wsm_ref, cw_ref, o_ref, osm_ref, h_ref, u_ref, *, seq):
    rows = x_ref.shape[0]
    g = g_ref[...]
    keep = jnp.where((pl.program_id(0) * rows) % seq == 0, 0.0, 1.0)
    h_ref[0:PROJ_HALO, :] = (_rmsnorm(halo_ref[...], g) * keep).astype(BF16)
    h = _rmsnorm(x_ref[...], g).astype(BF16)
    h_ref[PROJ_HALO:, :] = h
    osm_ref[...] = _dot(h, wsm_ref[...])

    n_conv_cols = cw_ref.shape[1]
    n_conv = n_conv_cols // PROJ_SUB
    n_plain = (w_ref.shape[1] - n_conv_cols) // PROJ_SUB
    off = PROJ_HALO - (CONV_B - 1)

    def project_conv(sub):
        u_ref[sub % 2] = _dot(h_ref[...], w_ref[:, sub * PROJ_SUB:(sub + 1) * PROJ_SUB])

    def project_plain(sub):
        cols = slice(n_conv_cols + sub * PROJ_SUB, n_conv_cols + (sub + 1) * PROJ_SUB)
        o_ref[:, cols] = _dot(h_ref[PROJ_HALO:, :], w_ref[:, cols]).astype(BF16)

    def epilogue(sub):
        kind = sub * PROJ_SUB // D_MODEL
        u = u_ref.at[sub % 2]
        for hh in range(PROJ_SUB // HEAD_W):
            lo = hh * HEAD_W
            col = sub * PROJ_SUB + lo
            last = CONV_B - 1
            y = cw_ref[last:CONV_B, col:col + HEAD_W] * u[off + last:off + last + rows, lo:lo + HEAD_W]
            for tap in range(last):
                y = y + cw_ref[tap:tap + 1, col:col + HEAD_W] * u[off + tap:off + tap + rows, lo:lo + HEAD_W]
            y = _silu(y)
            if kind < 2:
                y = y * (lax.rsqrt(jnp.sum(y * y, axis=-1, keepdims=True) + EPS) * (DK_B ** -0.5 if kind == 0 else 1.0))
            o_ref[:, col:col + HEAD_W] = y.astype(BF16)

    project_conv(0)
    for sub in range(max(n_conv, n_plain)):
        if sub + 1 < n_conv:
            project_conv(sub + 1)
        if sub < n_plain:
            project_plain(sub)
        if sub < n_conv:
            epilogue(sub)


def _gdn_proj(x2, g, w, wsm, conv_w, seq):
    rows = x2.shape[0]
    n_cols = w.shape[1]
    assert seq % GDN_PROJ_ROWS == 0 and GDN_PROJ_ROWS % PROJ_HALO == 0
    assert conv_w.shape[1] % PROJ_SUB == 0 and (n_cols - conv_w.shape[1]) % PROJ_SUB == 0 and D_MODEL % PROJ_SUB == 0
    halo_blocks = GDN_PROJ_ROWS // PROJ_HALO
    return pl.pallas_call(
        functools.partial(_gdn_proj_kernel, seq=seq),
        grid=(rows // GDN_PROJ_ROWS,),
        in_specs=[
            pl.BlockSpec((GDN_PROJ_ROWS, D_MODEL), lambda i: (i, 0)),
            pl.BlockSpec((PROJ_HALO, D_MODEL), lambda i: (jnp.maximum(i * halo_blocks - 1, 0), 0)),
            _resident((1, D_MODEL)),
            _resident((D_MODEL, n_cols)),
            _resident((D_MODEL, LANE)),
            _resident(conv_w.shape),
        ],
        out_specs=[
            pl.BlockSpec((GDN_PROJ_ROWS, n_cols), lambda i: (i, 0)),
            pl.BlockSpec((GDN_PROJ_ROWS, LANE), lambda i: (i, 0)),
        ],
        out_shape=[
            jax.ShapeDtypeStruct((rows, n_cols), BF16),
            jax.ShapeDtypeStruct((rows, LANE), F32),
        ],
        scratch_shapes=[pltpu.VMEM((PROJ_HALO + GDN_PROJ_ROWS, D_MODEL), BF16),
                        pltpu.VMEM((2, PROJ_HALO + GDN_PROJ_ROWS, PROJ_SUB), F32)],
        compiler_params=_params(("parallel",)),
        name="gdn_proj",
    )(x2, x2, g, w, wsm, conv_w)


def _bias_tiles_kernel(tab_ref, diag_ref, corner_ref):
    h = pl.program_id(0)
    t = diag_ref.shape[1]
    cn = corner_ref.shape[1]

    def bias_of(rel):
        b = jnp.full(rel.shape, tab_ref[h, 0], F32)
        for bk in range(1, N_BUCKETS):
            b = jnp.where(rel >= BUCKET_LO[bk - 1], tab_ref[h, bk], b)
        return b

    rel = lax.broadcasted_iota(jnp.int32, (t, t), 1) - lax.broadcasted_iota(jnp.int32, (t, t), 0)
    diag_ref[0] = jnp.where(rel >= 0, bias_of(rel), NEG_INF)
    rel_c = (lax.broadcasted_iota(jnp.int32, (cn, cn), 1) - lax.broadcasted_iota(jnp.int32, (cn, cn), 0)) + cn
    corner_ref[0] = bias_of(rel_c) - tab_ref[h, N_BUCKETS - 1]


def _bias_tiles(rel_bias_t):
    t = ATT_BLOCK
    assert ATT_CORNER + 1 >= BUCKET_LO[-1], "entries outside the corner must sit in the last bucket"
    return pl.pallas_call(
        _bias_tiles_kernel,
        grid=(N_HEADS,),
        in_specs=[pl.BlockSpec(memory_space=pltpu.SMEM)],
        out_specs=[
            pl.BlockSpec((1, t, t), lambda h: (h, 0, 0)),
            pl.BlockSpec((1, ATT_CORNER, ATT_CORNER), lambda h: (h, 0, 0)),
        ],
        out_shape=[
            jax.ShapeDtypeStruct((N_HEADS, t, t), F32),
            jax.ShapeDtypeStruct((N_HEADS, ATT_CORNER, ATT_CORNER), F32),
        ],
        compiler_params=_params(("parallel",)),
        name="bias_tiles",
    )(rel_bias_t)


def _attn_kernel(qi_tab, kj_tab, q_ref, k_ref, v_ref, diag_ref, corner_ref, tab_ref, lam_ref, sg_ref, li_ref,
                 o_ref, qs_ref, vt_ref, m_ref, acc_ref):
    hg = pl.program_id(0)
    step = pl.program_id(2)
    qi = qi_tab[step]
    kj = kj_tab[step]
    t = q_ref.shape[1]
    cn = corner_ref.shape[1]
    heads = range(ATT_HEADS)
    cols = [slice(g * HEAD_W, (g + 1) * HEAD_W) for g in heads]

    @pl.when(kj == 0)
    def _():
        lane = lax.broadcasted_iota(jnp.int32, (t, HEAD_W), 1)
        for g in heads:
            q = q_ref[0, :, cols[g]]
            zero = jnp.zeros_like(q)
            qs_ref[g, 0] = jnp.where(lane < HEAD_DIM_A, q, zero)
            qs_ref[g, 1] = jnp.where(lane >= HEAD_DIM_A, q, zero)
            vt_ref[g, HEAD_W:HEAD_W + ATT_ONES, :] = jnp.ones((ATT_ONES, t), BF16)
        m_ref[...] = jnp.full(m_ref.shape, NEG_INF, F32)
        acc_ref[...] = jnp.zeros(acc_ref.shape, F32)

    for g in heads:
        vt_ref[g, 0:HEAD_W, :] = v_ref[0, :, cols[g]].astype(F32).T.astype(BF16)
    delta = qi - kj

    def sweep(bias_fn, c):
        tq = t // ATT_QSPLIT
        units = [(g, j, slice(qp * tq, (qp + 1) * tq)) for g in heads for j in range(2) for qp in range(ATT_QSPLIT)]
        n = len(units)
        s, m_prev, m_new, p = {}, {}, {}, {}

        def logits(i):
            g, j, qcols = units[i]
            s[i] = bias_fn(_dot_nt(k_ref[0, :, cols[g]], qs_ref[g, j, qcols, :]), g, qcols)

        def softmax(i):
            g, j, qcols = units[i]
            m_prev[i] = m_ref[g, j, :, qcols]
            col_max = jnp.broadcast_to(jnp.max(s[i], axis=0, keepdims=True), (SUBLANE, tq))
            m_new[i] = jnp.maximum(m_prev[i], col_max + c[g])
            p[i] = jnp.exp2(s.pop(i) - (m_new[i][0:1, :] - c[g])).astype(BF16)

        def values(i):
            g, j, qcols = units[i]
            pv = _dot(vt_ref[g], p.pop(i))
            alpha = jnp.exp2(m_prev.pop(i) - m_new[i])
            acc_ref[g, j, :, qcols] = alpha[0:1, :] * acc_ref[g, j, :, qcols] + pv
            m_ref[g, j, :, qcols] = m_new.pop(i)

        for i in range(n + 2 * ATT_SKEW):
            if i < n:
                logits(i)
            if 0 <= i - ATT_SKEW < n:
                softmax(i - ATT_SKEW)
            if 0 <= i - 2 * ATT_SKEW < n:
                values(i - 2 * ATT_SKEW)

    @pl.when(delta >= 1)
    def _():
        near = jnp.where(delta == 1, 1.0, 0.0)

        def bias_fn(s, g, qcols):
            if qcols.start != 0:
                return s
            bottom = jnp.concatenate([s[t - cn:t, 0:cn] + near * corner_ref[g], s[t - cn:t, cn:]], axis=1)
            return jnp.concatenate([s[0:t - cn, :], bottom], axis=0)

        sweep(bias_fn, [tab_ref[hg * ATT_HEADS + g, N_BUCKETS - 1] for g in heads])

    @pl.when(delta == 0)
    def _():
        sweep(lambda s, g, qcols: s + diag_ref[g, :, qcols], [0.0] * ATT_HEADS)
        lq = lam_ref[...]
        lam_init = li_ref[0]
        lam = (jnp.exp(jnp.sum(lq[0:1] * lq[1:2], axis=-1, keepdims=True))
               - jnp.exp(jnp.sum(lq[2:3] * lq[3:4], axis=-1, keepdims=True)) + lam_init)
        for g in heads:
            a0 = acc_ref[g, 0]
            a1 = acc_ref[g, 1]
            o_t = (a0[0:HEAD_W] / a0[HEAD_W:HEAD_W + 1] - lam * (a1[0:HEAD_W] / a1[HEAD_W:HEAD_W + 1]))
            o_ref[0, :, cols[g]] = (_rmsnorm(o_t.T, sg_ref[...]) * (1.0 - lam_init)).astype(BF16)


def _attention(qkv, bias_diag, bias_corner, rel_bias_t, lam_a, subln, lam_init):
    bsz, seq, _ = qkv.shape
    t = ATT_BLOCK
    nq = seq // t
    assert ATT_CORNER <= t and N_HEADS % ATT_HEADS == 0
    n_groups = N_HEADS // ATT_HEADS
    gw = ATT_HEADS * HEAD_W
    pairs =[(qi, kj) for qi in range(nq) for kj in range(qi + 1)]
    qi_tab = jnp.asarray([p[0] for p in pairs], jnp.int32)
    kj_tab = jnp.asarray([p[1] for p in pairs], jnp.int32)
    grid_spec = pltpu.PrefetchScalarGridSpec(
        num_scalar_prefetch=2,
        grid=(n_groups, bsz, len(pairs)),
        in_specs=[
            pl.BlockSpec((1, t, gw), lambda h, b, s, qt, kt: (b, qt[s], h)),
            pl.BlockSpec((1, t, gw), lambda h, b, s, qt, kt: (b, kt[s], n_groups + h)),
            pl.BlockSpec((1, t, gw), lambda h, b, s, qt, kt: (b, kt[s], 2 * n_groups + h)),
            pl.BlockSpec((ATT_HEADS, t, t), lambda h, b, s, qt, kt: (h, 0, 0)),
            pl.BlockSpec((ATT_HEADS, ATT_CORNER, ATT_CORNER), lambda h, b, s, qt, kt: (h, 0, 0)),
            pl.BlockSpec(memory_space=pltpu.SMEM),
            pl.BlockSpec((4, HEAD_DIM_A), lambda h, b, s, qt, kt: (0, 0)),
            pl.BlockSpec((1, HEAD_W), lambda h, b, s, qt, kt: (0, 0)),
            pl.BlockSpec(memory_space=pltpu.SMEM),
        ],
        out_specs=pl.BlockSpec((1, t, gw), lambda h, b, s, qt, kt: (b, qt[s], h)),
        scratch_shapes=[
            pltpu.VMEM((ATT_HEADS, 2, t, HEAD_W), BF16),
            pltpu.VMEM((ATT_HEADS, HEAD_W + ATT_ONES, t), BF16),
            pltpu.VMEM((ATT_HEADS, 2, SUBLANE, t), F32),
            pltpu.VMEM((ATT_HEADS, 2, HEAD_W + ATT_ONES, t), F32),
        ],
    )
    return pl.pallas_call(
        _attn_kernel,
        grid_spec=grid_spec,
        out_shape=jax.ShapeDtypeStruct((bsz, seq, D_MODEL), BF16),
        compiler_params=_params(("parallel", "parallel", "arbitrary")),
        name="diff_attention",
    )(qi_tab, kj_tab, qkv, qkv, qkv, bias_diag, bias_corner, rel_bias_t, lam_a, subln, lam_init)


def _split(x, n_pieces):
    pieces = []
    for _ in range(n_pieces):
        piece = x.astype(BF16)
        pieces.append(piece)
        x = x - piece.astype(F32)
    return pieces


def _gdn_kernel(q_ref, k_ref, v_ref, z_ref, sm_ref, alog_ref, dtb_ref, on_ref, tril_ref, lvl_ref,
                o_ref, state_ref):
    c = GDN_CHUNK

    @pl.when(pl.program_id(1) == 0)
    def _():
        state_ref[...] = jnp.zeros(state_ref.shape, F32)

    chunks = range(q_ref.shape[1] // c)
    rows = [slice(ci * c, (ci + 1) * c) for ci in chunks]

    beta_all, gc_all, eg_all, ed_all = [], [], [], []
    for ci in chunks:
        sm = sm_ref[0, rows[ci], :]
        xa = sm + dtb_ref[...]
        softplus = jnp.maximum(xa, 0.0) + jnp.log1p(jnp.exp(-jnp.abs(xa)))
        g_all = -jnp.exp(alog_ref[...]) * softplus
        gc = _dot(tril_ref[...], jnp.concatenate(_split(g_all, 3), axis=0))
        beta_all.append(_sigmoid(sm))
        gc_all.append(gc)
        eg_all.append(jnp.exp(gc))
        ed_all.append(jnp.exp(gc[c - 1:c, :] - gc))

    ii = lax.broadcasted_iota(jnp.int32, (c, c), 0)
    jj = lax.broadcasted_iota(jnp.int32, (c, c), 1)
    tril = ii >= jj
    strict = ii > jj
    n_levels = lvl_ref.shape[0] - 1

    heads = range(N_HEADS)
    cols = [slice(h * HEAD_W, (h + 1) * HEAD_W) for h in heads]
    units = [(ci, h) for ci in chunks for h in heads]
    idx = range(len(units))
    lane_b = [slice(h, h + 1) for _, h in units]
    lane_g = [slice(N_HEADS + h, N_HEADS + h + 1) for _, h in units]
    at = [(0, rows[ci], cols[h]) for ci, h in units]
    beta = [beta_all[units[u][0]][:, lane_b[u]] for u in idx]
    gcr = [jnp.broadcast_to(gc_all[units[u][0]][:, lane_g[u]], (c, c)) for u in idx]
    eg = [eg_all[units[u][0]][:, lane_g[u]] for u in idx]
    ed = [ed_all[units[u][0]][:, lane_g[u]] for u in idx]
    k = [k_ref[at[u]] for u in idx]
    kq = [_dot_nt(jnp.concatenate([k[u], q_ref[at[u]]], axis=0), k[u]) for u in idx]
    decay = [jnp.where(tril, jnp.exp(gcr[u] - gcr[u].T), 0.0) for u in idx]
    a_b = [jnp.where(strict, kq[u][0:c] * (decay[u] * beta[u]), 0.0).astype(BF16) for u in idx]
    attn_b = [(kq[u][c:2 * c] * decay[u]).astype(BF16) for u in idx]
    inv = [lvl_ref[0] - a_b[u] * lvl_ref[1] for u in idx]
    for lv in range(2, n_levels + 1):
        x = [_dot(a_b[u] * lvl_ref[lv], inv[u]).astype(BF16) for u in idx]
        inv = [inv[u] - _dot(inv[u], x[u]).astype(BF16) for u in idx]
    rhs = [jnp.concatenate([k[u] * (beta[u] * eg[u]).astype(BF16), v_ref[at[u]] * beta[u].astype(BF16)], axis=1)
           for u in idx]
    wu = [_dot(inv[u], rhs[u]).astype(BF16) for u in idx]
    aw = [_dot(attn_b[u], wu[u]) for u in idx]
    pn = [_dot_tn(k[u] * ed[u].astype(BF16), wu[u]) for u in idx]
    q_eff = [q_ref[at[u]].astype(F32) * eg[u] - aw[u][:, 0:HEAD_W] for u in idx]
    state = [state_ref[h] for h in heads]
    for ci in chunks:
        us = [ci * N_HEADS + h for h in heads]
        r = [_dot(jnp.concatenate([pn[u][:, 0:HEAD_W], q_eff[u]], axis=0).astype(BF16), state[h].astype(BF16))
             for h, u in zip(heads, us)]
        state = [state[h] * eg[u][c - 1:c, :] + pn[u][:, HEAD_W:2 * HEAD_W] - r[h][0:DK_B] for h, u in zip(heads, us)]
        for h, u in zip(heads, us):
            o = r[h][DK_B:DK_B + c] + aw[u][:, HEAD_W:2 * HEAD_W]
            z = z_ref[at[u]].astype(F32)
            o_ref[at[u]] = (_rmsnorm(o, on_ref[...]) * _silu(z)).astype(BF16)
    for h in heads:
        state_ref[h] = state[h]


def _gdn_constants():
    c = GDN_CHUNK
    r = np.arange(c)
    tril = (r[None, :] <= r[:, None]).astype(np.float32)
    tril3 = np.concatenate([tril, tril, tril], axis=1)
    levels = [np.eye(c, dtype=np.float32)]
    s = 1
    while s < c:
        same = (r[:, None] ^ r[None, :]) < 2 * s
        levels.append((same & ((r[:, None] & s) != 0) & ((r[None, :] & s) == 0)).astype(np.float32))
        s *= 2
    return jnp.asarray(tril3, BF16), jnp.asarray(np.stack(levels), BF16)


def _gdn(proj_b, small, alog_row, dtb_row, onorm):
    bsz, seq, _ = proj_b.shape
    c = GDN_CHUNK
    step_rows = GDN_STEP_CHUNKS * c
    assert c == LANE and seq % step_rows == 0
    tril3, levels = _gdn_constants()
    blk = lambda col: pl.BlockSpec((1, step_rows, D_MODEL), lambda b, s: (b, s, col))
    return pl.pallas_call(
        _gdn_kernel,
        grid=(bsz, seq // step_rows),
        in_specs=[
            blk(0), blk(1), blk(2), blk(3),
            pl.BlockSpec((1, step_rows, LANE), lambda b, s: (b, s, 0)),
            pl.BlockSpec((1, LANE), lambda b, s: (0, 0)),
            pl.BlockSpec((1, LANE), lambda b, s: (0, 0)),
            pl.BlockSpec((1, HEAD_W), lambda b, s: (0, 0)),
            _resident(tril3.shape), _resident(levels.shape),
        ],
        out_specs=pl.BlockSpec((1, step_rows, D_MODEL), lambda b, s: (b, s, 0)),
        out_shape=jax.ShapeDtypeStruct((bsz, seq, D_MODEL), BF16),
        scratch_shapes=[pltpu.VMEM((N_HEADS, DK_B, HEAD_W), F32)],
        compiler_params=_params(("parallel", "arbitrary")),
        name="gated_delta_rule",
    )(proj_b, proj_b, proj_b, proj_b, small, alog_row, dtb_row, onorm, tril3, levels)


def _merge_kernel(x_ref, oa_ref, ob_ref, ga_ref, gb_ref, wa_ref, wb_ref, wo_ref, o_ref):
    ya = _dot(oa_ref[...], wa_ref[...])
    yb = _dot(ob_ref[...], wb_ref[...])
    mixed = _sigmoid(ga_ref[...].astype(F32)) * ya + _sigmoid(gb_ref[...].astype(F32)) * yb
    o_ref[...] = x_ref[...] + _dot(mixed.astype(BF16), wo_ref[...])


def _merge(x2, oa, ob, proj_b, wa, wb, wo):
    rows = x2.shape[0]
    gate_block = 4
    row_spec = pl.BlockSpec((MERGE_ROWS, D_MODEL), lambda i: (i, 0))
    return pl.pallas_call(
        _merge_kernel,
        grid=(rows // MERGE_ROWS,),
        in_specs=[
            row_spec, row_spec, row_spec,
            pl.BlockSpec((MERGE_ROWS, D_MODEL), lambda i: (i, gate_block)),
            pl.BlockSpec((MERGE_ROWS, D_MODEL), lambda i: (i, gate_block + 1)),
            _resident((D_MODEL, D_MODEL)), _resident((D_MODEL, D_MODEL)), _resident((D_MODEL, D_MODEL)),
        ],
        out_specs=row_spec,
        out_shape=jax.ShapeDtypeStruct((rows, D_MODEL), F32),
        compiler_params=_params(("parallel",)),
        name="branch_merge",
    )(x2, oa, ob, proj_b, proj_b, wa, wb, wo)


def _ffn_ple_kernel(x_ref, halo_ref, p_ref, fg_ref, wup_ref, cw_ref, cb_ref, wdn_ref, pg_ref, wg_ref, wp_ref,
                    o_ref, act_ref, *, seq):
    rows = x_ref.shape[0]
    x = x_ref[...]
    fg = fg_ref[...]
    keep = jnp.where((pl.program_id(0) * rows) % seq == 0, 0.0, 1.0)
    h_ext = jnp.concatenate([_rmsnorm(halo_ref[...], fg) * keep, _rmsnorm(x, fg)], axis=0).astype(BF16)
    off = HALO - (CONV_FF - 1)

    def conv(u, lo):
        last = CONV_FF - 1
        y = cb_ref[:, lo:lo + FFN_COLS] + cw_ref[last:CONV_FF, lo:lo + FFN_COLS] * u[off + last:off + last + rows]
        for tap in range(last):
            y = y + cw_ref[tap:tap + 1, lo:lo + FFN_COLS] * u[off + tap:off + tap + rows]
        return y

    for c in range(D_FF // FFN_COLS):
        lo = c * FFN_COLS
        yg = conv(_dot(h_ext, wup_ref[:, lo:lo + FFN_COLS]), lo)
        yv = conv(_dot(h_ext, wup_ref[:, D_FF + lo:D_FF + lo + FFN_COLS]), D_FF + lo)
        act_ref[:, lo:lo + FFN_COLS] = (_silu(yg) * yv).astype(BF16)

    x1 = x + _dot(act_ref[...], wdn_ref[...])
    h2 = _rmsnorm(x1, pg_ref[...]).astype(BF16)
    gate = _sigmoid(_dot(h2, wg_ref[...]))
    o_ref[...] = x1 + gate * _dot(p_ref[...].astype(BF16), wp_ref[...])


def _ffn_ple(x2, p2, fg, wup, cw, cb, wdn, pg, wg, wp, seq):
    rows = x2.shape[0]
    assert seq % FFN_ROWS == 0 and D_FF % FFN_COLS == 0
    halo_blocks = FFN_ROWS // HALO
    return pl.pallas_call(
        functools.partial(_ffn_ple_kernel, seq=seq),
        grid=(rows // FFN_ROWS,),
        in_specs=[
            pl.BlockSpec((FFN_ROWS, D_MODEL), lambda i: (i, 0)),
            pl.BlockSpec((HALO, D_MODEL), lambda i: (jnp.maximum(i * halo_blocks - 1, 0), 0)),
            pl.BlockSpec((FFN_ROWS, D_PLE), lambda i: (i, 0)),
            _resident((1, D_MODEL)),
            _resident((D_MODEL, 2 * D_FF)),
            _resident((CONV_FF, 2 * D_FF)),
            _resident((1, 2 * D_FF)),
            _resident((D_FF, D_MODEL)),
            _resident((1, D_MODEL)),
            _resident((D_MODEL, D_MODEL)),
            _resident((D_PLE, D_MODEL)),
        ],
        out_specs=pl.BlockSpec((FFN_ROWS, D_MODEL), lambda i: (i, 0)),
        out_shape=jax.ShapeDtypeStruct((rows, D_MODEL), F32),
        scratch_shapes=[pltpu.VMEM((FFN_ROWS, D_FF), BF16)],
        compiler_params=_params(("arbitrary",)),
        name="conv_ffn_ple",
    )(x2, x2, p2, fg, wup, cw, cb, wdn, pg, wg, wp)


def kernel(x, p, w_in, attn_norm, qnorm_a, knorm_a, lam_a, subln_a, rel_bias, conv_b, a_log_b, dt_bias_b, onorm_b, w_br_a, w_br_b, w_out, ffn_norm, w_up, conv_ff, conv_ff_bias, w_down, ple_norm, w_ple_gate, w_ple_proj):
    bsz, seq, _ = x.shape
    depth = w_in.shape[0]
    rows = bsz * seq
    assert rows % PROJ_ROWS == 0 and seq % ATT_BLOCK == 0 and rows % MERGE_ROWS == 0

    n_a = 3 * D_MODEL
    n_b = 4 * D_MODEL
    w_in_b = w_in.astype(BF16)
    rel_bias_t = rel_bias.astype(F32).T * LOG2E
    bias_diag, bias_corner = _bias_tiles(rel_bias_t)
    n_maps = D_MODEL // HEAD_DIM_A

    x2 = x.reshape(rows, D_MODEL).astype(F32)
    for i in range(depth):
        lam_init = 0.8 - 0.6 * math.exp(-0.3 * i)
        w_a = w_in_b[i, :, 0:n_a]
        w_b = jnp.concatenate([w_in_b[i, :, n_a:n_a + n_b], w_in_b[i, :, n_a + n_b + 2 * N_HEADS:]], axis=1)
        w_small = jnp.pad(w_in_b[i, :, n_a + n_b:n_a + n_b + 2 * N_HEADS], ((0, 0), (0, LANE - 2 * N_HEADS)))
        gains = jnp.stack([
            jnp.tile(qnorm_a[i].astype(F32), n_maps) * (HEAD_DIM_A ** -0.5 * LOG2E),
            jnp.tile(knorm_a[i].astype(F32), n_maps),
            jnp.ones((D_MODEL,), F32),
        ]).reshape(3, 1, D_MODEL)
        g_attn = attn_norm[i].astype(F32).reshape(1, D_MODEL)

        qkv = _attn_proj(x2, g_attn, w_a, gains)
        proj_b, small = _gdn_proj(x2, g_attn, w_b, w_small, conv_b[i].astype(F32), seq)

        oa = _attention(qkv.reshape(bsz, seq, n_a), bias_diag, bias_corner, rel_bias_t, lam_a[i].astype(F32),
                        subln_a[i].astype(F32).reshape(1, HEAD_W), jnp.full((1,), lam_init, F32))
        head_lanes = (N_HEADS, LANE - 2 * N_HEADS)
        ob = _gdn(proj_b.reshape(bsz, seq, -1), small.reshape(bsz, seq, LANE),
                  jnp.pad(a_log_b[i].astype(F32), head_lanes).reshape(1, LANE),
                  jnp.pad(dt_bias_b[i].astype(F32), head_lanes).reshape(1, LANE),
                  onorm_b[i].astype(F32).reshape(1, HEAD_W))

        x2 = _merge(x2, oa.reshape(rows, D_MODEL), ob.reshape(rows, D_MODEL), proj_b,
                    w_br_a[i].astype(BF16), w_br_b[i].astype(BF16), w_out[i].astype(BF16))

        x2 = _ffn_ple(x2, p[i].reshape(rows, D_PLE), ffn_norm[i].astype(F32).reshape(1, D_MODEL),
                      w_up[i].astype(BF16), conv_ff[i].astype(F32), conv_ff_bias[i].astype(F32).reshape(1, 2 * D_FF),
                      w_down[i].astype(BF16), ple_norm[i].astype(F32).reshape(1, D_MODEL),
                      w_ple_gate[i].astype(BF16), w_ple_proj[i].astype(BF16), seq)
    return x2.reshape(bsz, seq, D_MODEL).astype(x.dtype)
```
